```python
import jax, jax.numpy as jnp
from jax import lax
import numpy as np

D_MODEL = 1024
BATCH = 16
SEQ = 4096
DEPTH = 2
DEC_BATCH = 2
DEC_SEQ = 16384
PAST_LEN = 128

GRID_W = 64
MEM_TOKENS = 256
Q_BLOCK = 128
ROPE_THETA = 10000.0
EPS = 1e-6

A_HEADS = 8
A_KV_HEADS = 2
A_HEAD_DIM = 64
B_HEADS = 8
B_NOPE = 64
B_ROPE = 32
B_V = 64
B_Q_LORA = 256
B_KV_LORA = 128
B_QK = B_NOPE + B_ROPE
MIX_WIDTH = A_HEADS * A_HEAD_DIM + B_HEADS * B_V
A_Q_COLS = A_HEADS * A_HEAD_DIM
A_KV_COLS = A_KV_HEADS * A_HEAD_DIM
IN_SPLITS = [A_Q_COLS, A_KV_COLS, A_KV_COLS, B_Q_LORA, B_KV_LORA, B_ROPE]
IN_COLS = sum(IN_SPLITS)
M_HEADS = 4
M_HEAD_DIM = 128
N_GROUPS = 4
EXPERTS_PER_GROUP = 8
N_EXPERTS = N_GROUPS * EXPERTS_PER_GROUP
TOP_K = 2
D_EXPERT = 512
MOE_BLOCK = 256

kernel_name = 'hybrid_gqa_mla_memxattn_hiermoe_encoder'


def rms_norm(x, g):
    x32 = x.astype(jnp.float32)
    y = x32 * lax.rsqrt(jnp.mean(x32 * x32, axis=-1, keepdims=True) + EPS)
    return (y * g.astype(jnp.float32)).astype(x.dtype)


def axial_rope_tables(seq_len, rot_dim):
    rows = seq_len // GRID_W
    r = jnp.repeat(jnp.arange(rows, dtype=jnp.float32), GRID_W)
    c = jnp.tile(jnp.arange(GRID_W, dtype=jnp.float32), rows)
    axis_dim = rot_dim // 2
    inv = ROPE_THETA ** (-(jnp.arange(axis_dim // 2, dtype=jnp.float32) * 2.0) / axis_dim)
    ang = jnp.concatenate([r[:, None] * inv, c[:, None] * inv], axis=-1)
    return jnp.cos(ang), jnp.sin(ang)


def apply_rope(x, cos, sin):
    x1 = x[..., 0::2]
    x2 = x[..., 1::2]
    c = cos[None, :, None, :].astype(x.dtype)
    s = sin[None, :, None, :].astype(x.dtype)
    return jnp.stack([x1 * c - x2 * s, x1 * s + x2 * c], axis=-1).reshape(x.shape)


def blocked_attention(q, k, v):
    B, S, Hk, G, D = q.shape
    scale = D ** -0.5
    nb = S // Q_BLOCK
    qb = q.reshape(B, nb, Q_BLOCK, Hk, G, D).transpose(1, 0, 2, 3, 4, 5)

    def one_block(q_blk):
        s = jnp.einsum('bqhgd,bkhd->bhgqk', q_blk, k).astype(jnp.float32) * scale
        p = jax.nn.softmax(s, axis=-1)
        return jnp.einsum('bhgqk,bkhe->bqhge', p.astype(v.dtype), v)

    o = lax.map(one_block, qb)
    return o.transpose(1, 0, 2, 3, 4, 5).reshape(B, S, Hk * G, v.shape[-1])


def parallel_mixer(h, rope_a, rope_b, w_in, a_q_norm, a_k_norm, b_cq_norm, b_ckv_norm,
                   b_w_uq, b_w_ukv, b_q_norm, b_k_norm, w_out):
    B, S, _ = h.shape
    idx = np.cumsum(IN_SPLITS)[:-1].tolist()
    aq, ak, av, cq, ckv, kr = jnp.split(h @ w_in, idx, axis=-1)
    aq = apply_rope(rms_norm(aq.reshape(B, S, A_HEADS, A_HEAD_DIM), a_q_norm), *rope_a)
    ak = apply_rope(rms_norm(ak.reshape(B, S, A_KV_HEADS, A_HEAD_DIM), a_k_norm), *rope_a)
    av = av.reshape(B, S, A_KV_HEADS, A_HEAD_DIM)
    a_out = blocked_attention(
        aq.reshape(B, S, A_KV_HEADS, A_HEADS // A_KV_HEADS, A_HEAD_DIM), ak, av)
    a_out = a_out.reshape(B, S, A_HEADS * A_HEAD_DIM)
    bq = (rms_norm(cq, b_cq_norm) @ b_w_uq).reshape(B, S, B_HEADS, B_QK)
    kv = (rms_norm(ckv, b_ckv_norm) @ b_w_ukv).reshape(B, S, B_HEADS, B_NOPE + B_V)
    k_nope, bv = kv[..., :B_NOPE], kv[..., B_NOPE:]
    k_rope = jnp.broadcast_to(kr[:, :, None, :], (B, S, B_HEADS, B_ROPE))
    bk = jnp.concatenate([k_nope, k_rope], axis=-1)
    bq = rms_norm(bq, b_q_norm)
    bk = rms_norm(bk, b_k_norm)
    bq = jnp.concatenate([bq[..., :B_NOPE], apply_rope(bq[..., B_NOPE:], *rope_b)], axis=-1)
    bk = jnp.concatenate([bk[..., :B_NOPE], apply_rope(bk[..., B_NOPE:], *rope_b)], axis=-1)
    b_out = blocked_attention(bq.reshape(B, S, B_HEADS, 1, B_QK), bk, bv)
    b_out = b_out.reshape(B, S, B_HEADS * B_V)
    return jnp.concatenate([a_out, b_out], axis=-1) @ w_out


def memory_cross_attention(h, mem, m_norm, w_q, w_kv, q_norm, k_norm, w_o):
    B, S, _ = h.shape
    M = mem.shape[1]
    mn = rms_norm(mem, m_norm)
    q = rms_norm((h @ w_q).reshape(B, S, M_HEADS, M_HEAD_DIM), q_norm)
    kv = (mn @ w_kv).reshape(B, M, 2, M_HEADS, M_HEAD_DIM)
    k = rms_norm(kv[:, :, 0], k_norm)
    v = kv[:, :, 1]
    s = jnp.einsum('bshd,bmhd->bhsm', q, k).astype(jnp.float32) * (M_HEAD_DIM ** -0.5)
    p = jax.nn.softmax(s, axis=-1)
    o = jnp.einsum('bhsm,bmhd->bshd', p.astype(v.dtype), v).reshape(B, S, M_HEADS * M_HEAD_DIM)
    return o @ w_o


def hierarchical_moe(h, w_coarse, b_coarse, w_fine, b_fine, w_gate, w_up, w_down):
    B, S, D = h.shape
    N = B * S
    hf = h.reshape(N, D)
    coarse = (hf @ w_coarse).astype(jnp.float32) + b_coarse.astype(jnp.float32)
    coarse_p = jax.nn.softmax(coarse, axis=-1)
    g_idx = jnp.argmax(coarse, axis=-1).astype(jnp.int32)
    g_w = jnp.take_along_axis(coarse_p, g_idx[:, None], axis=-1)
    fine = ((hf @ w_fine).astype(jnp.float32) + b_fine.astype(jnp.float32))
    fine = fine.reshape(N, N_GROUPS, EXPERTS_PER_GROUP)
    fine = jnp.take_along_axis(fine, g_idx[:, None, None], axis=1)[:, 0]
    top_w, top_i = lax.top_k(jax.nn.softmax(fine, axis=-1), TOP_K)
    gate = g_w * top_w / jnp.sum(top_w, axis=-1, keepdims=True)
    expert = (g_idx[:, None] * EXPERTS_PER_GROUP + top_i).reshape(-1)
    gate = gate.reshape(-1)
    tok = jnp.repeat(jnp.arange(N, dtype=jnp.int32), TOP_K)
    M = N * TOP_K
    order = jnp.argsort(expert)
    e_sorted = expert[order]
    counts = jnp.bincount(expert, length=N_EXPERTS).astype(jnp.int32)
    padded = ((counts + MOE_BLOCK - 1) // MOE_BLOCK) * MOE_BLOCK
    padded_ends = jnp.cumsum(padded)
    padded_starts = padded_ends - padded
    starts = jnp.cumsum(counts) - counts
    dest = padded_starts[e_sorted] + (jnp.arange(M, dtype=jnp.int32) - starts[e_sorted])
    nblk = -(-(M + N_EXPERTS * (MOE_BLOCK - 1)) // MOE_BLOCK)
    P = nblk * MOE_BLOCK
    buf_tok = jnp.full((P,), N, dtype=jnp.int32).at[dest].set(tok[order])
    buf_gate = jnp.zeros((P,), jnp.float32).at[dest].set(gate[order])
    block_expert = jnp.clip(
        jnp.searchsorted(padded_ends, jnp.arange(nblk, dtype=jnp.int32) * MOE_BLOCK, side='right'),
        0, N_EXPERTS - 1)
    hpad = jnp.concatenate([hf, jnp.zeros((1, D), hf.dtype)], axis=0)
    xb = hpad[buf_tok].reshape(nblk, MOE_BLOCK, D)

    def expert_block(args):
        x_blk, e = args
        a = jax.nn.silu(x_blk @ w_gate[e]) * (x_blk @ w_up[e])
        return a @ w_down[e]

    yb = lax.map(expert_block, (xb, block_expert)).reshape(P, D)
    out = jnp.zeros((N + 1, D), jnp.float32).at[buf_tok].add(
        yb.astype(jnp.float32) * buf_gate[:, None])[:N]
    return out.astype(h.dtype).reshape(B, S, D)


def encoder_trunk(x, mem, p):
    S = x.shape[1]
    rope_a = axial_rope_tables(S, A_HEAD_DIM)
    rope_b = axial_rope_tables(S, B_ROPE)
    for i in range(DEPTH):
        h = rms_norm(x, p['norm_mix'][i])
        x = x + parallel_mixer(h, rope_a, rope_b, p['w_in'][i], p['a_q_norm'][i], p['a_k_norm'][i],
                               p['b_cq_norm'][i], p['b_ckv_norm'][i], p['b_w_uq'][i], p['b_w_ukv'][i],
                               p['b_q_norm'][i], p['b_k_norm'][i], p['w_out'][i])
        h = rms_norm(x, p['norm_mem_x'][i])
        x = x + memory_cross_attention(h, mem, p['norm_mem_m'][i], p['m_w_q'][i], p['m_w_kv'][i],
                                       p['m_q_norm'][i], p['m_k_norm'][i], p['m_w_o'][i])
        h = rms_norm(x, p['norm_ffn'][i])
        x = x + hierarchical_moe(h, p['r_coarse_w'][i], p['r_coarse_b'][i], p['r_fine_w'][i],
                                 p['r_fine_b'][i], p['e_w_gate'][i], p['e_w_up'][i], p['e_w_down'][i])
    return x


def setup_inputs(seed: int = 0) -> dict:
    key = jax.random.key(seed)
    ks = iter(jax.random.split(key, 40))

    def nrm(shape, scale):
        return scale * jax.random.normal(next(ks), shape, jnp.float32)

    def gain(n):
        return 1.0 + 0.02 * jax.random.normal(next(ks), (DEPTH, n), jnp.float32)

    L = DEPTH
    return {
        'x_prompt': nrm((BATCH, SEQ, D_MODEL), 1.0),
        'x_sample': nrm((DEC_BATCH, DEC_SEQ, D_MODEL), 1.0),
        'mem_prompt': nrm((BATCH, MEM_TOKENS, D_MODEL), 1.0),
        'mem_sample': nrm((DEC_BATCH, MEM_TOKENS, D_MODEL), 1.0),
        'norm_mix': gain(D_MODEL),
        'w_in': nrm((L, D_MODEL, IN_COLS), D_MODEL ** -0.5),
        'a_q_norm': gain(A_HEAD_DIM),
        'a_k_norm': gain(A_HEAD_DIM),
        'b_cq_norm': gain(B_Q_LORA),
        'b_ckv_norm': gain(B_KV_LORA),
        'b_w_uq': nrm((L, B_Q_LORA, B_HEADS * B_QK), B_Q_LORA ** -0.5),
        'b_w_ukv': nrm((L, B_KV_LORA, B_HEADS * (B_NOPE + B_V)), B_KV_LORA ** -0.5),
        'b_q_norm': gain(B_QK),
        'b_k_norm': gain(B_QK),
        'w_out': nrm((L, MIX_WIDTH, D_MODEL), MIX_WIDTH ** -0.5),
        'norm_mem_x': gain(D_MODEL),
        'norm_mem_m': gain(D_MODEL),
        'm_w_q': nrm((L, D_MODEL, M_HEADS * M_HEAD_DIM), D_MODEL ** -0.5),
        'm_w_kv': nrm((L, D_MODEL, 2 * M_HEADS * M_HEAD_DIM), D_MODEL ** -0.5),
        'm_q_norm': gain(M_HEAD_DIM),
        'm_k_norm': gain(M_HEAD_DIM),
        'm_w_o': nrm((L, M_HEADS * M_HEAD_DIM, D_MODEL), (M_HEADS * M_HEAD_DIM) ** -0.5),
        'norm_ffn': gain(D_MODEL),
        'r_coarse_w': nrm((L, D_MODEL, N_GROUPS), D_MODEL ** -0.5),
        'r_coarse_b': nrm((L, N_GROUPS), 0.01),
        'r_fine_w': nrm((L, D_MODEL, N_EXPERTS), D_MODEL ** -0.5),
        'r_fine_b': nrm((L, N_EXPERTS), 0.01),
        'e_w_gate': nrm((L, N_EXPERTS, D_MODEL, D_EXPERT), D_MODEL ** -0.5),
        'e_w_up': nrm((L, N_EXPERTS, D_MODEL, D_EXPERT), D_MODEL ** -0.5),
        'e_w_down': nrm((L, N_EXPERTS, D_EXPERT, D_MODEL), D_EXPERT ** -0.5),
    }


def reference(x_prompt, x_sample, mem_prompt, mem_sample, norm_mix, w_in, a_q_norm, a_k_norm,
              b_cq_norm, b_ckv_norm, b_w_uq, b_w_ukv, b_q_norm, b_k_norm, w_out,
              norm_mem_x, norm_mem_m, m_w_q, m_w_kv, m_q_norm, m_k_norm, m_w_o,
              norm_ffn, r_coarse_w, r_coarse_b, r_fine_w, r_fine_b, e_w_gate, e_w_up, e_w_down):
    p = dict(norm_mix=norm_mix, w_in=w_in, a_q_norm=a_q_norm, a_k_norm=a_k_norm,
             b_cq_norm=b_cq_norm, b_ckv_norm=b_ckv_norm, b_w_uq=b_w_uq, b_w_ukv=b_w_ukv,
             b_q_norm=b_q_norm, b_k_norm=b_k_norm, w_out=w_out,
             norm_mem_x=norm_mem_x, norm_mem_m=norm_mem_m, m_w_q=m_w_q, m_w_kv=m_w_kv,
             m_q_norm=m_q_norm, m_k_norm=m_k_norm, m_w_o=m_w_o,
             norm_ffn=norm_ffn, r_coarse_w=r_coarse_w, r_coarse_b=r_coarse_b,
             r_fine_w=r_fine_w, r_fine_b=r_fine_b, e_w_gate=e_w_gate, e_w_up=e_w_up,
             e_w_down=e_w_down)
    y_prompt = encoder_trunk(x_prompt, mem_prompt, p)
    y_sample = encoder_trunk(x_sample, mem_sample, p)
    return (y_prompt, y_sample)
```

```python
import functools

import numpy as np
import jax
import jax.numpy as jnp
from jax import lax
from jax.experimental import pallas as pl
from jax.experimental.pallas import tpu as pltpu

GRID_W = 64
ROPE_THETA = 10000.0
EPS = 1e-6
LOG2E = 1.4426950408889634

A_HEADS, A_KV_HEADS, A_HEAD_DIM = 8, 2, 64
B_HEADS, B_NOPE, B_ROPE, B_V = 8, 64, 32, 64
B_QK = B_NOPE + B_ROPE
B_Q_LORA, B_KV_LORA = 256, 128
M_HEADS, M_HEAD_DIM = 4, 128
N_GROUPS, EXPERTS_PER_GROUP, TOP_K = 4, 8, 2
N_EXPERTS = N_GROUPS * EXPERTS_PER_GROUP
LANES = 128

VMEM_LIMIT = 56 * 1024 * 1024

BF16 = jnp.bfloat16
F32 = jnp.float32


def _cparams(sem):
    return pltpu.CompilerParams(dimension_semantics=sem, vmem_limit_bytes=VMEM_LIMIT)


def _nt(a, b):
    return lax.dot_general(a, b, (((1,), (1,)), ((), ())), preferred_element_type=F32)


def _tn(a, b):
    return lax.dot_general(a, b, (((0,), (0,)), ((), ())), preferred_element_type=F32)


def _nn(a, b):
    return jnp.dot(a, b, preferred_element_type=F32)


def _rope_angles(seq_len, rot_dim):
    rows = seq_len // GRID_W
    r = jnp.repeat(jnp.arange(rows, dtype=F32), GRID_W)
    c = jnp.tile(jnp.arange(GRID_W, dtype=F32), rows)
    axis_dim = rot_dim // 2
    inv = ROPE_THETA ** (-(jnp.arange(axis_dim // 2, dtype=F32) * 2.0) / axis_dim)
    return jnp.concatenate([r[:, None] * inv, c[:, None] * inv], axis=-1)


def _rope_tables(seq_len):
    ang_a = _rope_angles(seq_len, A_HEAD_DIM)
    ang_b = _rope_angles(seq_len, B_ROPE)
    ca, sa = jnp.cos(ang_a), jnp.sin(ang_a)
    cb, sb = jnp.cos(ang_b), jnp.sin(ang_b)
    z64 = jnp.zeros((seq_len, 64), F32)
    z32 = jnp.zeros((seq_len, 32), F32)
    one64 = jnp.ones((seq_len, 64), F32)
    return dict(
        cos_a_t=ca.T, sin_a_t=sa.T,
        cos_b_t=cb.T, sin_b_t=sb.T,
        cos_ka=jnp.concatenate([ca, ca, z64], -1),
        sin_ka=jnp.concatenate([-sa, sa, z64], -1),
        cos_kb=jnp.concatenate([one64, cb, cb, z32], -1),
        sin_kb=jnp.concatenate([z64, -sb, sb, z32], -1),
    )


_EVEN64, _ODD64 = np.arange(0, 64, 2), np.arange(1, 64, 2)
_PERM64 = np.concatenate([_EVEN64, _ODD64])
_SWAP64 = np.concatenate([_ODD64, _EVEN64])
_EVEN32, _ODD32 = np.arange(0, 32, 2), np.arange(1, 32, 2)
_PERM32 = np.concatenate([_EVEN32, _ODD32])
_SWAP32 = np.concatenate([_ODD32, _EVEN32])


def _prep_layer(p, i):
    d_model = p['w_in'].shape[1]
    w_in = p['w_in'][i]
    o_ak = A_HEADS * A_HEAD_DIM
    o_av = o_ak + A_KV_HEADS * A_HEAD_DIM
    o_cq = o_av + A_KV_HEADS * A_HEAD_DIM
    o_ckv = o_cq + B_Q_LORA
    o_kr = o_ckv + B_KV_LORA
    zc64 = jnp.zeros((d_model, 64), F32)
    zc32 = jnp.zeros((d_model, 32), F32)

    aq_cols = np.concatenate([h * A_HEAD_DIM + _PERM64 for h in range(A_HEADS)])
    w_d = jnp.concatenate([w_in[:, aq_cols], w_in[:, o_av:o_cq], w_in[:, o_cq:o_ckv], w_in[:, o_ckv:o_kr]], -1)
    ak, akp = [], []
    for j in range(A_KV_HEADS):
        ak += [w_in[:, o_ak + j * A_HEAD_DIM + _PERM64], zc64]
        akp += [w_in[:, o_ak + j * A_HEAD_DIM + _SWAP64], zc64]
    kr = [zc64, w_in[:, o_kr + _PERM32], zc32]
    krp = [zc64, w_in[:, o_kr + _SWAP32], zc32]
    w_t = jnp.concatenate(ak + akp + [w_in[:, o_ckv:o_kr]] + kr + krp, -1)

    z64 = jnp.zeros((64,), F32)
    z32 = jnp.zeros((32,), F32)
    a_k = p['a_k_norm'][i]
    b_q = p['b_q_norm'][i]
    b_k = p['b_k_norm'][i]
    w_uq = p['b_w_uq'][i]
    uq_cols = np.concatenate([np.concatenate([h * B_QK + np.arange(B_NOPE), h * B_QK + B_NOPE + _PERM32])
                              for h in range(B_HEADS)])
    w_ukv = p['b_w_ukv'][i]
    zk = jnp.zeros((B_KV_LORA, 64), F32)
    wk_pad = jnp.concatenate(
        sum([[w_ukv[:, h * (B_NOPE + B_V):h * (B_NOPE + B_V) + B_NOPE], zk] for h in range(B_HEADS)], []), -1)
    wv = jnp.concatenate(
        [w_ukv[:, h * (B_NOPE + B_V) + B_NOPE:(h + 1) * (B_NOPE + B_V)] for h in range(B_HEADS)], -1)

    r_w = jnp.concatenate([p['r_coarse_w'][i], jnp.zeros((d_model, LANES - N_GROUPS), F32),
                           p['r_fine_w'][i], jnp.zeros((d_model, LANES - N_EXPERTS), F32)], -1)
    r_hi = r_w.astype(BF16)
    r_lo = (r_w - r_hi.astype(F32)).astype(BF16)
    r_b = jnp.concatenate([p['r_coarse_b'][i], jnp.zeros((LANES - N_GROUPS,), F32),
                           p['r_fine_b'][i], jnp.zeros((LANES - N_EXPERTS,), F32)])[None, :]
    return dict(
        g_mix=p['norm_mix'][i][None, :],
        w_d_t=w_d.T.astype(BF16),
        w_t=w_t.astype(BF16),
        g_aq=p['a_q_norm'][i][_PERM64][:, None],
        g_ak=jnp.concatenate([a_k[_PERM64], z64])[None, :],
        g_akp=jnp.concatenate([a_k[_SWAP64], z64])[None, :],
        g_cq=p['b_cq_norm'][i][:, None],
        g_ckv_row=p['b_ckv_norm'][i][None, :],
        g_ckv_col=p['b_ckv_norm'][i][:, None],
        w_uq_t=w_uq[:, uq_cols].T.astype(BF16),
        g_bq=jnp.concatenate([b_q[:B_NOPE], b_q[B_NOPE + _PERM32]])[:, None],
        wk_pad=wk_pad.astype(BF16),
        wv_t=wv.T.astype(BF16),
        g_bk=jnp.concatenate([b_k[:B_NOPE], b_k[B_NOPE + _PERM32], z32])[None, :],
        g_bkp=jnp.concatenate([z64, b_k[B_NOPE + _SWAP32], z32])[None, :],
        w_out=p['w_out'][i].astype(BF16),
        g_memx=p['norm_mem_x'][i][None, :],
        g_memm=p['norm_mem_m'][i][None, :],
        m_w_q=p['m_w_q'][i].astype(BF16),
        m_w_kv=p['m_w_kv'][i].astype(BF16),
        g_mq=p['m_q_norm'][i][None, :],
        g_mk=p['m_k_norm'][i][None, :],
        m_w_o=p['m_w_o'][i].astype(BF16),
        g_ffn=p['norm_ffn'][i][None, :],
        r_hi=r_hi, r_lo=r_lo, r_b=r_b,
        w_gu=jnp.concatenate([p['e_w_gate'][i], p['e_w_up'][i]], -1).astype(BF16),
        w_dn=p['e_w_down'][i].astype(BF16),
    )


def _in_proj_kernel(x_ref, g_mix, w_d_t, w_t, g_aq, g_ak, g_akp, g_cq, g_ckv_row, g_ckv_col, w_uq_t, g_bq,
                    wk_pad, wv_t, g_bk, g_bkp, cos_a_t, sin_a_t, cos_b_t, sin_b_t, cos_ka, sin_ka, cos_kb, sin_kb,
                    qa_ref, ka_ref, va_ref, qb_ref, kb_ref, vb_ref):
    x = x_ref[0]
    h = (x * lax.rsqrt(jnp.mean(x * x, axis=-1, keepdims=True) + EPS) * g_mix[...]).astype(BF16)
    pd = _nt(w_d_t[...], h)
    pt = _nn(h, w_t[...])
    ts = x.shape[0]

    ca, sa = cos_a_t[...], sin_a_t[...]
    qscale = A_HEAD_DIM ** -0.5 * LOG2E
    for hd in range(A_HEADS):
        q = pd[hd * 64:(hd + 1) * 64]
        qn = q * lax.rsqrt(jnp.mean(q * q, axis=0, keepdims=True) + EPS) * g_aq[...]
        x1, x2 = qn[:32], qn[32:]
        rot = jnp.concatenate([x1 * ca - x2 * sa, x1 * sa + x2 * ca], axis=0) * qscale
        qa_ref[0, hd] = rot.astype(BF16)
    o_av = A_HEADS * A_HEAD_DIM
    for j in range(A_KV_HEADS):
        va_ref[0, j] = pd[o_av + j * 64:o_av + (j + 1) * 64].astype(BF16)
    for j in range(A_KV_HEADS):
        k = pt[:, j * LANES:(j + 1) * LANES]
        kp = pt[:, (A_KV_HEADS + j) * LANES:(A_KV_HEADS + j + 1) * LANES]
        rs = lax.rsqrt(jnp.sum(k * k, axis=-1, keepdims=True) * (1.0 / A_HEAD_DIM) + EPS)
        ka_ref[0, j] = (rs * (k * g_ak[...] * cos_ka[...] + kp * g_akp[...] * sin_ka[...])).astype(BF16)

    o_cq = o_av + A_KV_HEADS * A_HEAD_DIM
    cq = pd[o_cq:o_cq + B_Q_LORA]
    cqn = (cq * lax.rsqrt(jnp.mean(cq * cq, axis=0, keepdims=True) + EPS) * g_cq[...]).astype(BF16)
    bq = _nn(w_uq_t[...], cqn)
    cb, sb = cos_b_t[...], sin_b_t[...]
    bscale = B_QK ** -0.5 * LOG2E
    zpad = jnp.zeros((LANES - B_QK, ts), F32)
    for hd in range(B_HEADS):
        q = bq[hd * B_QK:(hd + 1) * B_QK]
        qn = q * lax.rsqrt(jnp.mean(q * q, axis=0, keepdims=True) + EPS) * g_bq[...]
        x1, x2 = qn[B_NOPE:B_NOPE + 16], qn[B_NOPE + 16:]
        rot = jnp.concatenate([qn[:B_NOPE], x1 * cb - x2 * sb, x1 * sb + x2 * cb], axis=0) * bscale
        qb_ref[0, hd] = jnp.concatenate([rot, zpad], axis=0).astype(BF16)

    o_ckv_t = 2 * A_KV_HEADS * LANES
    ckv = pt[:, o_ckv_t:o_ckv_t + B_KV_LORA]
    ckvn = (ckv * lax.rsqrt(jnp.mean(ckv * ckv, axis=-1, keepdims=True) + EPS) * g_ckv_row[...]).astype(BF16)
    k_nope = _nn(ckvn, wk_pad[...])
    kr = pt[:, o_ckv_t + LANES:o_ckv_t + 2 * LANES]
    krp = pt[:, o_ckv_t + 2 * LANES:o_ckv_t + 3 * LANES]
    rope_part = krp * g_bkp[...] * sin_kb[...]
    for hd in range(B_HEADS):
        k = k_nope[:, hd * LANES:(hd + 1) * LANES] + kr
        rs = lax.rsqrt(jnp.sum(k * k, axis=-1, keepdims=True) * (1.0 / B_QK) + EPS)
        kb_ref[0, hd] = (rs * (k * g_bk[...] * cos_kb[...] + rope_part)).astype(BF16)
    o_ckv = o_cq + B_Q_LORA
    ckv_t = pd[o_ckv:o_ckv + B_KV_LORA]
    ckvn_t = (ckv_t * lax.rsqrt(jnp.mean(ckv_t * ckv_t, axis=0, keepdims=True) + EPS) * g_ckv_col[...]).astype(BF16)
    bv = _nn(wv_t[...], ckvn_t)
    for hd in range(B_HEADS):
        vb_ref[0, hd] = bv[hd * B_V:(hd + 1) * B_V].astype(BF16)


def _in_proj(x, lw, rt, ts):
    bsz, seq, d_model = x.shape
    grid = (bsz, seq // ts)
    full = lambda a: pl.BlockSpec(a.shape, lambda b, j: (0,) * a.ndim)
    weights = [lw[k] for k in ('g_mix', 'w_d_t', 'w_t', 'g_aq', 'g_ak', 'g_akp', 'g_cq', 'g_ckv_row', 'g_ckv_col',
                               'w_uq_t', 'g_bq', 'wk_pad', 'wv_t', 'g_bk', 'g_bkp')]
    tabs_t = [rt[k] for k in ('cos_a_t', 'sin_a_t', 'cos_b_t', 'sin_b_t')]
    tabs_k = [rt[k] for k in ('cos_ka', 'sin_ka', 'cos_kb', 'sin_kb')]
    in_specs = ([pl.BlockSpec((1, ts, d_model), lambda b, j: (b, j, 0))] + [full(w) for w in weights]
                + [pl.BlockSpec((t.shape[0], ts), lambda b, j: (0, j)) for t in tabs_t]
                + [pl.BlockSpec((ts, LANES), lambda b, j: (j, 0)) for _ in tabs_k])
    t_major = lambda nh: pl.BlockSpec((1, nh, ts, LANES), lambda b, j: (b, 0, j, 0))
    d_major = lambda nh, d: pl.BlockSpec((1, nh, d, ts), lambda b, j: (b, 0, 0, j))
    out_shape = [
        jax.ShapeDtypeStruct((bsz, A_HEADS, A_HEAD_DIM, seq), BF16),
        jax.ShapeDtypeStruct((bsz, A_KV_HEADS, seq, LANES), BF16),
        jax.ShapeDtypeStruct((bsz, A_KV_HEADS, A_HEAD_DIM, seq), BF16),
        jax.ShapeDtypeStruct((bsz, B_HEADS, LANES, seq), BF16),
        jax.ShapeDtypeStruct((bsz, B_HEADS, seq, LANES), BF16),
        jax.ShapeDtypeStruct((bsz, B_HEADS, B_V, seq), BF16),
    ]
    out_specs = [d_major(A_HEADS, A_HEAD_DIM), t_major(A_KV_HEADS), d_major(A_KV_HEADS, A_HEAD_DIM),
                 d_major(B_HEADS, LANES), t_major(B_HEADS), d_major(B_HEADS, B_V)]
    return pl.pallas_call(
        _in_proj_kernel, grid=grid, in_specs=in_specs, out_specs=out_specs, out_shape=out_shape,
        compiler_params=_cparams(("parallel", "parallel")), name="in_proj",
    )(x, *weights, *tabs_t, *tabs_k)


def _attn_kernel(q_ref, k_ref, v_ref, o_ref, *, dq, tk, wp):
    n_heads, _, tqb = q_ref.shape[1:]
    seq = k_ref.shape[2]
    dv = v_ref.shape[2]
    for g in range(n_heads):
        for pnl in range(tqb // wp):
            q = q_ref[0, g, :, pnl * wp:(pnl + 1) * wp]

            def body(c, carry, q=q):
                m, l, acc = carry
                off = pl.multiple_of(c * tk, tk)
                k = k_ref[0, 0, pl.ds(off, tk), :]
                s = _nn(k[:, :dq], q)
                m_new = jnp.maximum(m, jnp.max(s, axis=0, keepdims=True))
                p = jnp.exp2(s - m_new)
                alpha = jnp.exp2(m - m_new)
                l = alpha * l + jnp.sum(p, axis=0, keepdims=True)
                v = v_ref[0, 0, :, pl.ds(off, tk)]
                acc = alpha * acc + _nn(v, p.astype(BF16))
                return m_new, l, acc

            init = (jnp.full((1, wp), -jnp.inf, F32), jnp.zeros((1, wp), F32), jnp.zeros((dv, wp), F32))
            _, l, acc = lax.fori_loop(0, seq // tk, body, init)
            o_ref[0, g, :, pnl * wp:(pnl + 1) * wp] = (acc * (1.0 / l)).astype(BF16)


def _attention(q_t, k, v_t, *, dq, tqb, tk, wp, name):
    bsz, hq, dq_rows, seq = q_t.shape
    hk = k.shape[1]
    grp = hq // hk
    dv = v_t.shape[2]
    grid = (bsz, hk, seq // tqb)
    return pl.pallas_call(
        functools.partial(_attn_kernel, dq=dq, tk=tk, wp=wp),
        grid=grid,
        in_specs=[pl.BlockSpec((1, grp, dq_rows, tqb), lambda b, j, i: (b, j, 0, i)),
                  pl.BlockSpec((1, 1, seq, LANES), lambda b, j, i: (b, j, 0, 0)),
                  pl.BlockSpec((1, 1, dv, seq), lambda b, j, i: (b, j, 0, 0))],
        out_specs=pl.BlockSpec((1, grp, dv, tqb), lambda b, j, i: (b, j, 0, i)),
        out_shape=jax.ShapeDtypeStruct((bsz, hq, dv, seq), BF16),
        compiler_params=_cparams(("parallel", "parallel", "parallel")), name=name,
    )(q_t, k, v_t)


def _mem_kv_kernel(mem_ref, g_memm, w_kv, g_mk, k_ref, v_ref):
    m = mem_ref[0]
    mn = (m * lax.rsqrt(jnp.mean(m * m, axis=-1, keepdims=True) + EPS) * g_memm[...]).astype(BF16)
    kv = _nn(mn, w_kv[...])
    hw = M_HEADS * M_HEAD_DIM
    for hh in range(M_HEADS):
        k = kv[:, hh * M_HEAD_DIM:(hh + 1) * M_HEAD_DIM]
        kn = k * lax.rsqrt(jnp.mean(k * k, axis=-1, keepdims=True) + EPS) * g_mk[...]
        k_ref[0, :, hh * M_HEAD_DIM:(hh + 1) * M_HEAD_DIM] = kn.astype(BF16)
    v_ref[0] = kv[:, hw:].astype(BF16)


def _mem_kv(mem, lw):
    bsz, n_mem, d_model = mem.shape
    hw = M_HEADS * M_HEAD_DIM
    full = lambda a: pl.BlockSpec(a.shape, lambda b: (0,) * a.ndim)
    ws = [lw['g_memm'], lw['m_w_kv'], lw['g_mk']]
    return pl.pallas_call(
        _mem_kv_kernel, grid=(bsz,),
        in_specs=[pl.BlockSpec((1, n_mem, d_model), lambda b: (b, 0, 0))] + [full(w) for w in ws],
        out_specs=[pl.BlockSpec((1, n_mem, hw), lambda b: (b, 0, 0))] * 2,
        out_shape=[jax.ShapeDtypeStruct((bsz, n_mem, hw), BF16)] * 2,
        compiler_params=_cparams(("parallel",)), name="mem_kv",
    )(mem, *ws)


def _pack_bf16_pairs(lo, hi):
    lo_b = lax.bitcast_convert_type(lo.astype(BF16).astype(F32), jnp.uint32)
    hi_b = lax.bitcast_convert_type(hi.astype(BF16).astype(F32), jnp.uint32)
    return (hi_b & jnp.uint32(0xFFFF0000)) | (lo_b >> 16)


def _post_attn_kernel(x_ref, oa_ref, ob_ref, mk_ref, mv_ref, w_out, g_memx, m_w_q, g_mq, m_w_o, g_ffn,
                      r_hi, r_lo, r_b, x_out, h_out, route_out, cnt_out, cnt_scr):
    first = jnp.logical_and(pl.program_id(0) == 0, pl.program_id(1) == 0)

    @pl.when(first)
    def _():
        cnt_scr[...] = jnp.zeros_like(cnt_scr)

    ts = x_ref.shape[1]
    half = A_HEADS * A_HEAD_DIM
    oa = oa_ref[0].reshape(half, ts)
    ob = ob_ref[0].reshape(half, ts)
    x1 = x_ref[0] + _tn(oa, w_out[:half, :]) + _tn(ob, w_out[half:, :])

    h2 = (x1 * lax.rsqrt(jnp.mean(x1 * x1, axis=-1, keepdims=True) + EPS) * g_memx[...]).astype(BF16)
    qm = _nn(h2, m_w_q[...])
    mscale = M_HEAD_DIM ** -0.5 * LOG2E
    heads = []
    for hh in range(M_HEADS):
        sl = slice(hh * M_HEAD_DIM, (hh + 1) * M_HEAD_DIM)
        q = qm[:, sl]
        qn = (q * lax.rsqrt(jnp.mean(q * q, axis=-1, keepdims=True) + EPS) * g_mq[...] * mscale).astype(BF16)
        s = _nt(qn, mk_ref[0, :, sl])
        p = jnp.exp2(s - jnp.max(s, axis=-1, keepdims=True))
        o = _nn(p.astype(BF16), mv_ref[0, :, sl]) * (1.0 / jnp.sum(p, axis=-1, keepdims=True))
        heads.append(o.astype(BF16))
    x2 = x1 + _nn(jnp.concatenate(heads, axis=-1), m_w_o[...])
    x_out[0] = x2

    h3 = x2 * lax.rsqrt(jnp.mean(x2 * x2, axis=-1, keepdims=True) + EPS) * g_ffn[...]
    d_half = h3.shape[1] // 2
    h_out[0] = _pack_bf16_pairs(h3[:, :d_half], h3[:, d_half:])

    h_hi = h3.astype(BF16)
    h_lo = (h3 - h_hi.astype(F32)).astype(BF16)
    logits = _nn(h_hi, r_hi[...]) + _nn(h_hi, r_lo[...]) + _nn(h_lo, r_hi[...]) + r_b[...]
    lane = lax.broadcasted_iota(jnp.int32, (ts, LANES), 1)
    neg = jnp.float32(-jnp.inf)
    lc = jnp.where(lane < N_GROUPS, logits[:, :LANES], neg)
    mc = jnp.max(lc, axis=-1, keepdims=True)
    g_idx = jnp.min(jnp.where(lc == mc, lane, LANES), axis=-1, keepdims=True)
    g_w = 1.0 / jnp.sum(jnp.exp(lc - mc), axis=-1, keepdims=True)
    lf = jnp.where((lane >> 3) == g_idx, logits[:, LANES:], neg)
    m1 = jnp.max(lf, axis=-1, keepdims=True)
    i1 = jnp.min(jnp.where(lf == m1, lane, LANES), axis=-1, keepdims=True)
    lf2 = jnp.where(lane == i1, neg, lf)
    m2 = jnp.max(lf2, axis=-1, keepdims=True)
    i2 = jnp.min(jnp.where(lf2 == m2, lane, LANES), axis=-1, keepdims=True)
    r21 = jnp.exp(m2 - m1)
    gate1 = g_w / (1.0 + r21)
    gate2 = g_w * r21 / (1.0 + r21)

    hot1 = lane == i1
    hot2 = lane == i2
    onehot = jnp.where(jnp.logical_or(hot1, hot2), 1.0, 0.0)
    row = lax.broadcasted_iota(jnp.int32, (ts, ts), 0)
    col = lax.broadcasted_iota(jnp.int32, (ts, ts), 1)
    tri = jnp.where(col < row, 1.0, 0.0).astype(BF16)
    base = _nn(tri, onehot.astype(BF16)) + cnt_scr[...]
    rank1 = jnp.sum(jnp.where(hot1, base, 0.0), axis=-1, keepdims=True)
    rank2 = jnp.sum(jnp.where(hot2, base, 0.0), axis=-1, keepdims=True)
    cnt_scr[...] += jnp.sum(onehot, axis=0, keepdims=True)
    cnt_out[...] = cnt_scr[...]

    vals = (i1.astype(F32), i2.astype(F32), gate1, gate2, rank1, rank2)
    route = jnp.zeros((ts, LANES), F32)
    for n, v in enumerate(vals):
        route = jnp.where(lane == n, v, route)
    route_out[0] = route


def _post_attn(x, oa, ob, mk, mv, lw, ts):
    bsz, seq, d_model = x.shape
    n_mem, hw = mk.shape[1:]
    grid = (bsz, seq // ts)
    full = lambda a: pl.BlockSpec(a.shape, lambda b, j: (0,) * a.ndim)
    ws = [lw[k] for k in ('w_out', 'g_memx', 'm_w_q', 'g_mq', 'm_w_o', 'g_ffn', 'r_hi', 'r_lo', 'r_b')]
    tok = lambda w: pl.BlockSpec((1, ts, w), lambda b, j: (b, j, 0))
    return pl.pallas_call(
        _post_attn_kernel, grid=grid,
        in_specs=[tok(d_model),
                  pl.BlockSpec((1, A_HEADS, A_HEAD_DIM, ts), lambda b, j: (b, 0, 0, j)),
                  pl.BlockSpec((1, B_HEADS, B_V, ts), lambda b, j: (b, 0, 0, j)),
                  pl.BlockSpec((1, n_mem, hw), lambda b, j: (b, 0, 0)),
                  pl.BlockSpec((1, n_mem, hw), lambda b, j: (b, 0, 0))] + [full(w) for w in ws],
        out_specs=[tok(d_model), tok(d_model // 2), tok(LANES), pl.BlockSpec((1, LANES), lambda b, j: (0, 0))],
        out_shape=[jax.ShapeDtypeStruct((bsz, seq, d_model), F32),
                   jax.ShapeDtypeStruct((bsz, seq, d_model // 2), jnp.uint32),
                   jax.ShapeDtypeStruct((bsz, seq, LANES), F32),
                   jax.ShapeDtypeStruct((1, LANES), F32)],
        scratch_shapes=[pltpu.VMEM((1, LANES), F32)],
        compiler_params=_cparams(("arbitrary", "arbitrary")), name="post_attn",
    )(x, oa, ob, mk, mv, *ws)


def _row_copy(src, dst, sem, s_row, d_row):
    return pltpu.make_async_copy(src.at[pl.ds(s_row, 1)], dst.at[pl.ds(d_row, 1)], sem)


def _dispatch_kernel(dest_ref, h_ref, xs_in, xs_out, sem, *, td):
    del xs_in
    base = pl.program_id(0) * td

    def issue(t, c):
        for kk in range(TOP_K):
            _row_copy(h_ref, xs_out, sem, base + t, dest_ref[0, 0, TOP_K * t + kk]).start()
        return c

    lax.fori_loop(0, td, issue, 0)

    def drain(t, c):
        for kk in range(TOP_K):
            _row_copy(h_ref, xs_out, sem, 0, 0).wait()
        return c

    lax.fori_loop(0, td, drain, 0)


def _dispatch(h_packed, dest, n_rows, td):
    n_tok, width = h_packed.shape
    dest3 = dest.reshape(n_tok // td, 1, TOP_K * td)
    xs0 = jnp.zeros((n_rows, width), h_packed.dtype)
    return pl.pallas_call(
        functools.partial(_dispatch_kernel, td=td), grid=(n_tok // td,),
        in_specs=[pl.BlockSpec((1, 1, TOP_K * td), lambda i: (i, 0, 0), memory_space=pltpu.SMEM),
                  pl.BlockSpec(memory_space=pl.ANY), pl.BlockSpec(memory_space=pl.ANY)],
        out_specs=pl.BlockSpec(memory_space=pl.ANY),
        out_shape=jax.ShapeDtypeStruct((n_rows, width), h_packed.dtype),
        scratch_shapes=[pltpu.SemaphoreType.DMA(())],
        input_output_aliases={2: 0},
        compiler_params=_cparams(("arbitrary",)), name="moe_dispatch",
    )(dest3, h_packed, xs0)


def _experts_kernel(be_ref, nu_ref, xs_ref, w_gu, w_dn, ys_ref):
    i = pl.program_id(0)

    @pl.when(i < nu_ref[0])
    def _():
        packed = xs_ref[...]
        lo = lax.bitcast_convert_type(packed << 16, F32)
        hi = lax.bitcast_convert_type(packed & jnp.uint32(0xFFFF0000), F32)
        xb = jnp.concatenate([lo, hi], axis=-1).astype(BF16)
        gu = _nn(xb, w_gu[0])
        d_e = gu.shape[1] // 2
        g = gu[:, :d_e]
        a = g * (1.0 / (1.0 + jnp.exp(-g))) * gu[:, d_e:]
        ys_ref[...] = _nn(a.astype(BF16), w_dn[0])

    @pl.when(i >= nu_ref[0])
    def _():
        ys_ref[...] = jnp.zeros_like(ys_ref)


def _experts(xs, block_expert, n_used, lw, bm):
    n_rows, width = xs.shape
    d_model = 2 * width
    w_gu, w_dn = lw['w_gu'], lw['w_dn']
    grid_spec = pltpu.PrefetchScalarGridSpec(
        num_scalar_prefetch=2, grid=(n_rows // bm,),
        in_specs=[pl.BlockSpec((bm, width), lambda i, be, nu: (i, 0)),
                  pl.BlockSpec((1,) + w_gu.shape[1:], lambda i, be, nu: (be[i], 0, 0)),
                  pl.BlockSpec((1,) + w_dn.shape[1:], lambda i, be, nu: (be[i], 0, 0))],
        out_specs=pl.BlockSpec((bm, d_model), lambda i, be, nu: (i, 0)))
    return pl.pallas_call(
        _experts_kernel, grid_spec=grid_spec,
        out_shape=jax.ShapeDtypeStruct((n_rows, d_model), F32),
        compiler_params=_cparams(("arbitrary",)), name="moe_experts",
    )(block_expert, n_used, xs, w_gu, w_dn)


def _combine_kernel(dest_ref, ys_ref, x_ref, route_ref, o_ref, buf, sem, *, tc):
    def issue(t, c):
        for kk in range(TOP_K):
            pltpu.make_async_copy(ys_ref.at[pl.ds(dest_ref[0, 0, TOP_K * t + kk], 1)],
                                  buf.at[kk, pl.ds(t, 1)], sem).start()
        return c

    lax.fori_loop(0, tc, issue, 0)

    def drain(t, c):
        for kk in range(TOP_K):
            pltpu.make_async_copy(ys_ref.at[pl.ds(0, 1)], buf.at[kk, pl.ds(0, 1)], sem).wait()
        return c

    lax.fori_loop(0, tc, drain, 0)
    route = route_ref[...]
    o_ref[...] = x_ref[...] + route[:, 2:3] * buf[0] + route[:, 3:4] * buf[1]


def _combine(ys, dest, x, route, tc):
    n_tok, d_model = x.shape
    dest3 = dest.reshape(n_tok // tc, 1, TOP_K * tc)
    return pl.pallas_call(
        functools.partial(_combine_kernel, tc=tc), grid=(n_tok // tc,),
        in_specs=[pl.BlockSpec((1, 1, TOP_K * tc), lambda i: (i, 0, 0), memory_space=pltpu.SMEM),
                  pl.BlockSpec(memory_space=pl.ANY),
                  pl.BlockSpec((tc, d_model), lambda i: (i, 0)),
                  pl.BlockSpec((tc, LANES), lambda i: (i, 0))],
        out_specs=pl.BlockSpec((tc, d_model), lambda i: (i, 0)),
        out_shape=jax.ShapeDtypeStruct((n_tok, d_model), F32),
        scratch_shapes=[pltpu.VMEM((TOP_K, tc, d_model), F32), pltpu.SemaphoreType.DMA(())],
        compiler_params=_cparams(("arbitrary",)), name="moe_combine",
    )(dest3, ys, x, route)


def _moe(x2, h_packed, route, counts, lw, cfg):
    bsz, seq, d_model = x2.shape
    n_tok = bsz * seq
    bm = cfg['bm']
    cnt = counts[0, :N_EXPERTS].astype(jnp.int32)
    padded = ((cnt + bm - 1) // bm) * bm
    ends = jnp.cumsum(padded)
    starts = ends - padded
    n_blk = -(-(n_tok * TOP_K + N_EXPERTS * (bm - 1)) // bm)
    route2 = route.reshape(n_tok, LANES)
    expert = route2[:, 0:TOP_K].astype(jnp.int32)
    rank = route2[:, 4:4 + TOP_K].astype(jnp.int32)
    dest = (starts[expert] + rank).reshape(-1)
    block_expert = jnp.clip(
        jnp.searchsorted(ends, jnp.arange(n_blk, dtype=jnp.int32) * bm, side='right'), 0, N_EXPERTS - 1
    ).astype(jnp.int32)
    n_used = (ends[-1:] // bm).astype(jnp.int32)
    xs = _dispatch(h_packed.reshape(n_tok, d_model // 2), dest, n_blk * bm, cfg['td'])
    ys = _experts(xs, block_expert, n_used, lw, bm)
    out = _combine(ys, dest, x2.reshape(n_tok, d_model), route2, cfg['tc'])
    return out.reshape(bsz, seq, d_model)


def _config(seq):
    if seq % 2048 == 0:
        return dict(ts=512, tq_a=512, tq_b=2048, tk=512, wp=512, bm=256, td=256, tc=256)
    return dict(ts=128, tq_a=128, tq_b=256, tk=128, wp=128, bm=64, td=64, tc=64)


def _trunk(x, mem, layers, cfg):
    rt = _rope_tables(x.shape[1])
    for lw in layers:
        qa, ka, va, qb, kb, vb = _in_proj(x, lw, rt, cfg['ts'])
        oa = _attention(qa, ka, va, dq=A_HEAD_DIM, tqb=cfg['tq_a'], tk=cfg['tk'], wp=cfg['wp'], name="attn_a")
        ob = _attention(qb, kb, vb, dq=LANES, tqb=cfg['tq_b'], tk=cfg['tk'], wp=cfg['wp'], name="attn_b")
        mk, mv = _mem_kv(mem, lw)
        x2, h_packed, route, counts = _post_attn(x, oa, ob, mk, mv, lw, cfg['ts'])
        x = _moe(x2, h_packed, route, counts, lw, cfg)
    return x


def kernel(x_prompt, x_sample, mem_prompt, mem_sample, norm_mix, w_in, a_q_norm, a_k_norm, b_cq_norm, b_ckv_norm, b_w_uq, b_w_ukv, b_q_norm, b_k_norm, w_out, norm_mem_x, norm_mem_m, m_w_q, m_w_kv, m_q_norm, m_k_norm, m_w_o, norm_ffn, r_coarse_w, r_coarse_b, r_fine_w, r_fine_b, e_w_gate, e_w_up, e_w_down):
    p = dict(norm_mix=norm_mix, w_in=w_in, a_q_norm=a_q_norm, a_k_norm=a_k_norm, b_cq_norm=b_cq_norm,
             b_ckv_norm=b_ckv_norm, b_w_uq=b_w_uq, b_w_ukv=b_w_ukv, b_q_norm=b_q_norm, b_k_norm=b_k_norm,
             w_out=w_out, norm_mem_x=norm_mem_x, norm_mem_m=norm_mem_m, m_w_q=m_w_q, m_w_kv=m_w_kv,
             m_q_norm=m_q_norm, m_k_norm=m_k_norm, m_w_o=m_w_o, norm_ffn=norm_ffn, r_coarse_w=r_coarse_w,
             r_coarse_b=r_coarse_b, r_fine_w=r_fine_w, r_fine_b=r_fine_b, e_w_gate=e_w_gate, e_w_up=e_w_up,
             e_w_down=e_w_down)
    layers = [_prep_layer(p, i) for i in range(w_in.shape[0])]
    y_prompt = _trunk(x_prompt, mem_prompt, layers, _config(x_prompt.shape[1]))
    y_sample = _trunk(x_sample, mem_sample, layers, _config(x_sample.shape[1]))
    return (y_prompt, y_sample)
```

```python
import functools

import numpy as np
import jax
import jax.numpy as jnp
from jax import lax
from jax.experimental import pallas as pl
from jax.experimental.pallas import tpu as pltpu

GRID_W = 64
ROPE_THETA = 10000.0
EPS = 1e-6
LOG2E = 1.4426950408889634

A_HEADS, A_KV_HEADS, A_HEAD_DIM = 8, 2, 64
B_HEADS, B_NOPE, B_ROPE, B_V = 8, 64, 32, 64
B_QK = B_NOPE + B_ROPE
B_Q_LORA, B_KV_LORA = 256, 128
M_HEADS, M_HEAD_DIM = 4, 128
N_GROUPS, EXPERTS_PER_GROUP, TOP_K = 4, 8, 2
N_EXPERTS = N_GROUPS * EXPERTS_PER_GROUP
LANES = 128
SUM_ROWS = 16

VMEM_LIMIT = 56 * 1024 * 1024

BF16 = jnp.bfloat16
F32 = jnp.float32


def _cparams(sem):
    return pltpu.CompilerParams(dimension_semantics=sem, vmem_limit_bytes=VMEM_LIMIT)


def _nt(a, b):
    return lax.dot_general(a, b, (((1,), (1,)), ((), ())), preferred_element_type=F32)


def _tn(a, b):
    return lax.dot_general(a, b, (((0,), (0,)), ((), ())), preferred_element_type=F32)


def _nn(a, b):
    return jnp.dot(a, b, preferred_element_type=F32)


def _rope_angles(seq_len, rot_dim):
    rows = seq_len // GRID_W
    r = jnp.repeat(jnp.arange(rows, dtype=F32), GRID_W)
    c = jnp.tile(jnp.arange(GRID_W, dtype=F32), rows)
    axis_dim = rot_dim // 2
    inv = ROPE_THETA ** (-(jnp.arange(axis_dim // 2, dtype=F32) * 2.0) / axis_dim)
    return jnp.concatenate([r[:, None] * inv, c[:, None] * inv], axis=-1)


def _rope_tables(seq_len):
    ang_a = _rope_angles(seq_len, A_HEAD_DIM)
    ang_b = _rope_angles(seq_len, B_ROPE)
    ca, sa = jnp.cos(ang_a), jnp.sin(ang_a)
    cb, sb = jnp.cos(ang_b), jnp.sin(ang_b)
    z64 = jnp.zeros((seq_len, 64), F32)
    z32 = jnp.zeros((seq_len, 32), F32)
    one64 = jnp.ones((seq_len, 64), F32)
    return dict(
        cos_a_t=ca.T, sin_a_t=sa.T,
        cos_b_t=cb.T, sin_b_t=sb.T,
        cos_ka=jnp.concatenate([ca, ca, z64], -1),
        sin_ka=jnp.concatenate([-sa, sa, z64], -1),
        cos_kb=jnp.concatenate([one64, cb, cb, z32], -1),
        sin_kb=jnp.concatenate([z64, -sb, sb, z32], -1),
    )


_EVEN64, _ODD64 = np.arange(0, 64, 2), np.arange(1, 64, 2)
_PERM64 = np.concatenate([_EVEN64, _ODD64])
_SWAP64 = np.concatenate([_ODD64, _EVEN64])
_EVEN32, _ODD32 = np.arange(0, 32, 2), np.arange(1, 32, 2)
_PERM32 = np.concatenate([_EVEN32, _ODD32])
_SWAP32 = np.concatenate([_ODD32, _EVEN32])


def _prep_layer(p, i):
    d_model = p['w_in'].shape[1]
    w_in = p['w_in'][i]
    o_ak = A_HEADS * A_HEAD_DIM
    o_av = o_ak + A_KV_HEADS * A_HEAD_DIM
    o_cq = o_av + A_KV_HEADS * A_HEAD_DIM
    o_ckv = o_cq + B_Q_LORA
    o_kr = o_ckv + B_KV_LORA
    zc64 = jnp.zeros((d_model, 64), F32)
    zc32 = jnp.zeros((d_model, 32), F32)

    aq_cols = np.concatenate([h * A_HEAD_DIM + _PERM64 for h in range(A_HEADS)])
    w_d = jnp.concatenate([w_in[:, aq_cols], w_in[:, o_av:o_cq], w_in[:, o_cq:o_ckv], w_in[:, o_ckv:o_kr]], -1)
    ak, akp = [], []
    for j in range(A_KV_HEADS):
        ak += [w_in[:, o_ak + j * A_HEAD_DIM + _PERM64], zc64]
        akp += [w_in[:, o_ak + j * A_HEAD_DIM + _SWAP64], zc64]
    kr = [zc64, w_in[:, o_kr + _PERM32], zc32]
    krp = [zc64, w_in[:, o_kr + _SWAP32], zc32]
    w_t = jnp.concatenate(ak + akp + [w_in[:, o_ckv:o_kr]] + kr + krp, -1)

    z64 = jnp.zeros((64,), F32)
    z32 = jnp.zeros((32,), F32)
    a_k = p['a_k_norm'][i]
    b_q = p['b_q_norm'][i]
    b_k = p['b_k_norm'][i]
    w_uq = p['b_w_uq'][i]
    uq_cols = np.concatenate([np.concatenate([h * B_QK + np.arange(B_NOPE), h * B_QK + B_NOPE + _PERM32])
                              for h in range(B_HEADS)])
    w_ukv = p['b_w_ukv'][i]
    zk = jnp.zeros((B_KV_LORA, 64), F32)
    wk_pad = jnp.concatenate(
        sum([[w_ukv[:, h * (B_NOPE + B_V):h * (B_NOPE + B_V) + B_NOPE], zk] for h in range(B_HEADS)], []), -1)
    wv = jnp.concatenate(
        [w_ukv[:, h * (B_NOPE + B_V) + B_NOPE:(h + 1) * (B_NOPE + B_V)] for h in range(B_HEADS)], -1)

    r_w = jnp.concatenate([p['r_coarse_w'][i], jnp.zeros((d_model, LANES - N_GROUPS), F32),
                           p['r_fine_w'][i], jnp.zeros((d_model, LANES - N_EXPERTS), F32)], -1)
    r_hi = r_w.astype(BF16)
    r_lo = (r_w - r_hi.astype(F32)).astype(BF16)
    r_b = jnp.concatenate([p['r_coarse_b'][i], jnp.zeros((LANES - N_GROUPS,), F32),
                           p['r_fine_b'][i], jnp.zeros((LANES - N_EXPERTS,), F32)])[None, :]
    return dict(
        g_mix=p['norm_mix'][i][None, :],
        w_d_t=w_d.T.astype(BF16),
        w_t=w_t.astype(BF16),
        g_aq=p['a_q_norm'][i][_PERM64][:, None],
        g_ak=jnp.concatenate([a_k[_PERM64], z64])[None, :],
        g_akp=jnp.concatenate([a_k[_SWAP64], z64])[None, :],
        g_cq=p['b_cq_norm'][i][:, None],
        g_ckv_row=p['b_ckv_norm'][i][None, :],
        g_ckv_col=p['b_ckv_norm'][i][:, None],
        w_uq_t=w_uq[:, uq_cols].T.astype(BF16),
        g_bq=jnp.concatenate([b_q[:B_NOPE], b_q[B_NOPE + _PERM32]])[:, None],
        wk_pad=wk_pad.astype(BF16),
        wv_t=wv.T.astype(BF16),
        g_bk=jnp.concatenate([b_k[:B_NOPE], b_k[B_NOPE + _PERM32], z32])[None, :],
        g_bkp=jnp.concatenate([z64, b_k[B_NOPE + _SWAP32], z32])[None, :],
        w_out=p['w_out'][i].astype(BF16),
        g_memx=p['norm_mem_x'][i][None, :],
        g_memm=p['norm_mem_m'][i][None, :],
        m_w_q=p['m_w_q'][i].astype(BF16),
        m_w_kv=p['m_w_kv'][i].astype(BF16),
        g_mq=p['m_q_norm'][i][None, :],
        g_mk=p['m_k_norm'][i][None, :],
        m_w_o=p['m_w_o'][i].astype(BF16),
        g_ffn=p['norm_ffn'][i][None, :],
        r_hi=r_hi, r_lo=r_lo, r_b=r_b,
        w_gu=jnp.concatenate([p['e_w_gate'][i], p['e_w_up'][i]], -1).astype(BF16),
        w_dn=p['e_w_down'][i].astype(BF16),
    )


def _in_proj_kernel(x_ref, g_mix, w_d_t, w_t, g_aq, g_ak, g_akp, g_cq, g_ckv_row, g_ckv_col, w_uq_t, g_bq,
                    wk_pad, wv_t, g_bk, g_bkp, cos_a_t, sin_a_t, cos_b_t, sin_b_t, cos_ka, sin_ka, cos_kb, sin_kb,
                    qa_ref, ka_ref, va_ref, qb_ref, kb_ref, vb_ref):
    x = x_ref[0]
    h = (x * lax.rsqrt(jnp.mean(x * x, axis=-1, keepdims=True) + EPS) * g_mix[...]).astype(BF16)
    pd = _nt(w_d_t[...], h)
    pt = _nn(h, w_t[...])
    ts = x.shape[0]

    ca, sa = cos_a_t[...], sin_a_t[...]
    qscale = A_HEAD_DIM ** -0.5 * LOG2E
    for hd in range(A_HEADS):
        q = pd[hd * 64:(hd + 1) * 64]
        qn = q * lax.rsqrt(jnp.mean(q * q, axis=0, keepdims=True) + EPS) * g_aq[...]
        x1, x2 = qn[:32], qn[32:]
        rot = jnp.concatenate([x1 * ca - x2 * sa, x1 * sa + x2 * ca], axis=0) * qscale
        qa_ref[0, hd] = rot.astype(BF16)
    o_av = A_HEADS * A_HEAD_DIM
    for j in range(A_KV_HEADS):
        va_ref[0, j] = pd[o_av + j * 64:o_av + (j + 1) * 64].astype(BF16)
    for j in range(A_KV_HEADS):
        k = pt[:, j * LANES:(j + 1) * LANES]
        kp = pt[:, (A_KV_HEADS + j) * LANES:(A_KV_HEADS + j + 1) * LANES]
        rs = lax.rsqrt(jnp.sum(k * k, axis=-1, keepdims=True) * (1.0 / A_HEAD_DIM) + EPS)
        ka_ref[0, j] = (rs * (k * g_ak[...] * cos_ka[...] + kp * g_akp[...] * sin_ka[...])).astype(BF16)

    o_cq = o_av + A_KV_HEADS * A_HEAD_DIM
    cq = pd[o_cq:o_cq + B_Q_LORA]
    cqn = (cq * lax.rsqrt(jnp.mean(cq * cq, axis=0, keepdims=True) + EPS) * g_cq[...]).astype(BF16)
    bq = _nn(w_uq_t[...], cqn)
    cb, sb = cos_b_t[...], sin_b_t[...]
    bscale = B_QK ** -0.5 * LOG2E
    zpad = jnp.zeros((LANES - B_QK, ts), F32)
    for hd in range(B_HEADS):
        q = bq[hd * B_QK:(hd + 1) * B_QK]
        qn = q * lax.rsqrt(jnp.mean(q * q, axis=0, keepdims=True) + EPS) * g_bq[...]
        x1, x2 = qn[B_NOPE:B_NOPE + 16], qn[B_NOPE + 16:]
        rot = jnp.concatenate([qn[:B_NOPE], x1 * cb - x2 * sb, x1 * sb + x2 * cb], axis=0) * bscale
        qb_ref[0, hd] = jnp.concatenate([rot, zpad], axis=0).astype(BF16)

    o_ckv_t = 2 * A_KV_HEADS * LANES
    ckv = pt[:, o_ckv_t:o_ckv_t + B_KV_LORA]
    ckvn = (ckv * lax.rsqrt(jnp.mean(ckv * ckv, axis=-1, keepdims=True) + EPS) * g_ckv_row[...]).astype(BF16)
    k_nope = _nn(ckvn, wk_pad[...])
    kr = pt[:, o_ckv_t + LANES:o_ckv_t + 2 * LANES]
    krp = pt[:, o_ckv_t + 2 * LANES:o_ckv_t + 3 * LANES]
    rope_part = krp * g_bkp[...] * sin_kb[...]
    for hd in range(B_HEADS):
        k = k_nope[:, hd * LANES:(hd + 1) * LANES] + kr
        rs = lax.rsqrt(jnp.sum(k * k, axis=-1, keepdims=True) * (1.0 / B_QK) + EPS)
        kb_ref[0, hd] = (rs * (k * g_bk[...] * cos_kb[...] + rope_part)).astype(BF16)
    o_ckv = o_cq + B_Q_LORA
    ckv_t = pd[o_ckv:o_ckv + B_KV_LORA]
    ckvn_t = (ckv_t * lax.rsqrt(jnp.mean(ckv_t * ckv_t, axis=0, keepdims=True) + EPS) * g_ckv_col[...]).astype(BF16)
    bv = _nn(wv_t[...], ckvn_t)
    for hd in range(B_HEADS):
        vb_ref[0, hd] = bv[hd * B_V:(hd + 1) * B_V].astype(BF16)


def _in_proj(x, lw, rt, ts):
    bsz, seq, d_model = x.shape
    grid = (bsz, seq // ts)
    full = lambda a: pl.BlockSpec(a.shape, lambda b, j: (0,) * a.ndim)
    weights = [lw[k] for k in ('g_mix', 'w_d_t', 'w_t', 'g_aq', 'g_ak', 'g_akp', 'g_cq', 'g_ckv_row', 'g_ckv_col',
                               'w_uq_t', 'g_bq', 'wk_pad', 'wv_t', 'g_bk', 'g_bkp')]
    tabs_t = [rt[k] for k in ('cos_a_t', 'sin_a_t', 'cos_b_t', 'sin_b_t')]
    tabs_k = [rt[k] for k in ('cos_ka', 'sin_ka', 'cos_kb', 'sin_kb')]
    in_specs = ([pl.BlockSpec((1, ts, d_model), lambda b, j: (b, j, 0))] + [full(w) for w in weights]
                + [pl.BlockSpec((t.shape[0], ts), lambda b, j: (0, j)) for t in tabs_t]
                + [pl.BlockSpec((ts, LANES), lambda b, j: (j, 0)) for _ in tabs_k])
    t_major = lambda nh: pl.BlockSpec((1, nh, ts, LANES), lambda b, j: (b, 0, j, 0))
    d_major = lambda nh, d: pl.BlockSpec((1, nh, d, ts), lambda b, j: (b, 0, 0, j))
    out_shape = [
        jax.ShapeDtypeStruct((bsz, A_HEADS, A_HEAD_DIM, seq), BF16),
        jax.ShapeDtypeStruct((bsz, A_KV_HEADS, seq, LANES), BF16),
        jax.ShapeDtypeStruct((bsz, A_KV_HEADS, A_HEAD_DIM, seq), BF16),
        jax.ShapeDtypeStruct((bsz, B_HEADS, LANES, seq), BF16),
        jax.ShapeDtypeStruct((bsz, B_HEADS, seq, LANES), BF16),
        jax.ShapeDtypeStruct((bsz, B_HEADS, B_V, seq), BF16),
    ]
    out_specs = [d_major(A_HEADS, A_HEAD_DIM), t_major(A_KV_HEADS), d_major(A_KV_HEADS, A_HEAD_DIM),
                 d_major(B_HEADS, LANES), t_major(B_HEADS), d_major(B_HEADS, B_V)]
    return pl.pallas_call(
        _in_proj_kernel, grid=grid, in_specs=in_specs, out_specs=out_specs, out_shape=out_shape,
        compiler_params=_cparams(("parallel", "parallel")), name="in_proj",
    )(x, *weights, *tabs_t, *tabs_k)


def _attn_kernel(q_ref, k_ref, v_ref, o_ref, s_scr, cmax_scr, *, dq, tk, wp):
    n_heads, _, tqb = q_ref.shape[1:]
    seq = k_ref.shape[2]
    dv = v_ref.shape[2]
    panels = [(g, pnl * wp) for g in range(n_heads) for pnl in range(tqb // wp)]
    n_chunks = seq // tk

    def scores(c, slot):
        off = pl.multiple_of(c * tk, tk)
        k = k_ref[0, 0, pl.ds(off, tk), :][:, :dq]
        for n, (g, lo) in enumerate(panels):
            s = _nn(k, q_ref[0, g, :, lo:lo + wp])
            s_scr[slot, n] = s
            cmax_scr[slot, n] = jnp.max(s, axis=0, keepdims=True)

    ones_rows = jnp.ones((SUM_ROWS, tk), BF16)

    def update(c, slot, state):
        off = pl.multiple_of(c * tk, tk)
        v = jnp.concatenate([v_ref[0, 0, :, pl.ds(off, tk)], ones_rows], axis=0)
        out = []
        for n, (m, acc) in enumerate(state):
            m_new = jnp.maximum(m, cmax_scr[slot, n])
            p = jnp.exp2((s_scr[slot, n] - m_new).astype(BF16))
            acc = jnp.exp2(m - m_new) * acc + _nn(v, p)
            out.append((m_new, acc))
        return tuple(out)

    def body(j, state):
        c = 2 * j
        scores(c + 1, 1)
        state = update(c, 0, state)
        scores(c + 2, 0)
        return update(c + 1, 1, state)

    state = tuple((jnp.full((1, wp), -jnp.inf, F32), jnp.zeros((dv + SUM_ROWS, wp), F32)) for _ in panels)
    scores(0, 0)
    state = lax.fori_loop(0, n_chunks // 2 - 1, body, state)
    scores(n_chunks - 1, 1)
    state = update(n_chunks - 2, 0, state)
    state = update(n_chunks - 1, 1, state)
    for (g, lo), (_, acc) in zip(panels, state):
        o_ref[0, g, :, lo:lo + wp] = (acc[:dv] * (1.0 / acc[dv:dv + 1])).astype(BF16)


def _attention(q_t, k, v_t, *, dq, tqb, tk, wp, name):
    bsz, hq, dq_rows, seq = q_t.shape
    hk = k.shape[1]
    grp = hq // hk
    dv = v_t.shape[2]
    grid = (bsz, hk, seq // tqb)
    return pl.pallas_call(
        functools.partial(_attn_kernel, dq=dq, tk=tk, wp=wp),
        grid=grid,
        in_specs=[pl.BlockSpec((1, grp, dq_rows, tqb), lambda b, j, i: (b, j, 0, i)),
                  pl.BlockSpec((1, 1, seq, LANES), lambda b, j, i: (b, j, 0, 0)),
                  pl.BlockSpec((1, 1, dv, seq), lambda b, j, i: (b, j, 0, 0))],
        out_specs=pl.BlockSpec((1, grp, dv, tqb), lambda b, j, i: (b, j, 0, i)),
        out_shape=jax.ShapeDtypeStruct((bsz, hq, dv, seq), BF16),
        scratch_shapes=[pltpu.VMEM((2, grp * (tqb // wp), tk, wp), F32),
                        pltpu.VMEM((2, grp * (tqb // wp), 1, wp), F32)],
        compiler_params=_cparams(("parallel", "parallel", "parallel")), name=name,
    )(q_t, k, v_t)


def _mem_kv_kernel(mem_ref, g_memm, w_kv, g_mk, k_ref, v_ref):
    m = mem_ref[0]
    mn = (m * lax.rsqrt(jnp.mean(m * m, axis=-1, keepdims=True) + EPS) * g_memm[...]).astype(BF16)
    kv = _nn(mn, w_kv[...])
    hw = M_HEADS * M_HEAD_DIM
    for hh in range(M_HEADS):
        k = kv[:, hh * M_HEAD_DIM:(hh + 1) * M_HEAD_DIM]
        kn = k * lax.rsqrt(jnp.mean(k * k, axis=-1, keepdims=True) + EPS) * g_mk[...]
        k_ref[0, :, hh * M_HEAD_DIM:(hh + 1) * M_HEAD_DIM] = kn.astype(BF16)
    v_ref[0] = kv[:, hw:].astype(BF16)


def _mem_kv(mem, lw):
    bsz, n_mem, d_model = mem.shape
    hw = M_HEADS * M_HEAD_DIM
    full = lambda a: pl.BlockSpec(a.shape, lambda b: (0,) * a.ndim)
    ws = [lw['g_memm'], lw['m_w_kv'], lw['g_mk']]
    return pl.pallas_call(
        _mem_kv_kernel, grid=(bsz,),
        in_specs=[pl.BlockSpec((1, n_mem, d_model), lambda b: (b, 0, 0))] + [full(w) for w in ws],
        out_specs=[pl.BlockSpec((1, n_mem, hw), lambda b: (b, 0, 0))] * 2,
        out_shape=[jax.ShapeDtypeStruct((bsz, n_mem, hw), BF16)] * 2,
        compiler_params=_cparams(("parallel",)), name="mem_kv",
    )(mem, *ws)


def _pack_bf16_pairs(lo, hi):
    lo_b = lax.bitcast_convert_type(lo.astype(BF16).astype(F32), jnp.uint32)
    hi_b = lax.bitcast_convert_type(hi.astype(BF16).astype(F32), jnp.uint32)
    return (hi_b & jnp.uint32(0xFFFF0000)) | (lo_b >> 16)


def _post_attn_kernel(x_ref, oa_ref, ob_ref, mk_ref, mv_ref, w_out, g_memx, m_w_q, g_mq, m_w_o, g_ffn,
                      r_hi, r_lo, r_b, x_out, h_out, route_out, cnt_out, cnt_scr):
    first = jnp.logical_and(pl.program_id(0) == 0, pl.program_id(1) == 0)

    @pl.when(first)
    def _():
        cnt_scr[...] = jnp.zeros_like(cnt_scr)

    ts = x_ref.shape[1]
    half = A_HEADS * A_HEAD_DIM
    oa = oa_ref[0].reshape(half, ts)
    ob = ob_ref[0].reshape(half, ts)
    x1 = x_ref[0] + _tn(oa, w_out[:half, :]) + _tn(ob, w_out[half:, :])

    h2 = (x1 * lax.rsqrt(jnp.mean(x1 * x1, axis=-1, keepdims=True) + EPS) * g_memx[...]).astype(BF16)
    qm = _nn(h2, m_w_q[...])
    mscale = M_HEAD_DIM ** -0.5 * LOG2E
    heads = []
    for hh in range(M_HEADS):
        sl = slice(hh * M_HEAD_DIM, (hh + 1) * M_HEAD_DIM)
        q = qm[:, sl]
        qn = (q * lax.rsqrt(jnp.mean(q * q, axis=-1, keepdims=True) + EPS) * g_mq[...] * mscale).astype(BF16)
        s = _nt(qn, mk_ref[0, :, sl])
        p = jnp.exp2(s - jnp.max(s, axis=-1, keepdims=True))
        o = _nn(p.astype(BF16), mv_ref[0, :, sl]) * (1.0 / jnp.sum(p, axis=-1, keepdims=True))
        heads.append(o.astype(BF16))
    x2 = x1 + _nn(jnp.concatenate(heads, axis=-1), m_w_o[...])
    x_out[0] = x2

    h3 = x2 * lax.rsqrt(jnp.mean(x2 * x2, axis=-1, keepdims=True) + EPS) * g_ffn[...]
    d_half = h3.shape[1] // 2
    h_out[0] = _pack_bf16_pairs(h3[:, :d_half], h3[:, d_half:])

    h_hi = h3.astype(BF16)
    h_lo = (h3 - h_hi.astype(F32)).astype(BF16)
    logits = _nn(h_hi, r_hi[...]) + _nn(h_hi, r_lo[...]) + _nn(h_lo, r_hi[...]) + r_b[...]
    lane = lax.broadcasted_iota(jnp.int32, (ts, LANES), 1)
    neg = jnp.float32(-jnp.inf)
    lc = jnp.where(lane < N_GROUPS, logits[:, :LANES], neg)
    mc = jnp.max(lc, axis=-1, keepdims=True)
    g_idx = jnp.min(jnp.where(lc == mc, lane, LANES), axis=-1, keepdims=True)
    g_w = 1.0 / jnp.sum(jnp.exp(lc - mc), axis=-1, keepdims=True)
    lf = jnp.where((lane >> 3) == g_idx, logits[:, LANES:], neg)
    m1 = jnp.max(lf, axis=-1, keepdims=True)
    i1 = jnp.min(jnp.where(lf == m1, lane, LANES), axis=-1, keepdims=True)
    lf2 = jnp.where(lane == i1, neg, lf)
    m2 = jnp.max(lf2, axis=-1, keepdims=True)
    i2 = jnp.min(jnp.where(lf2 == m2, lane, LANES), axis=-1, keepdims=True)
    r21 = jnp.exp(m2 - m1)
    gate1 = g_w / (1.0 + r21)
    gate2 = g_w * r21 / (1.0 + r21)

    hot1 = lane == i1
    hot2 = lane == i2
    onehot = jnp.where(jnp.logical_or(hot1, hot2), 1.0, 0.0)
    row = lax.broadcasted_iota(jnp.int32, (ts, ts), 0)
    col = lax.broadcasted_iota(jnp.int32, (ts, ts), 1)
    tri = jnp.where(col < row, 1.0, 0.0).astype(BF16)
    base = _nn(tri, onehot.astype(BF16)) + cnt_scr[...]
    rank1 = jnp.sum(jnp.where(hot1, base, 0.0), axis=-1, keepdims=True)
    rank2 = jnp.sum(jnp.where(hot2, base, 0.0), axis=-1, keepdims=True)
    cnt_scr[...] += jnp.sum(onehot, axis=0, keepdims=True)
    cnt_out[...] = cnt_scr[...]

    vals = (i1.astype(F32), i2.astype(F32), gate1, gate2, rank1, rank2)
    route = jnp.zeros((ts, LANES), F32)
    for n, v in enumerate(vals):
        route = jnp.where(lane == n, v, route)
    route_out[0] = route


def _post_attn(x, oa, ob, mk, mv, lw, ts):
    bsz, seq, d_model = x.shape
    n_mem, hw = mk.shape[1:]
    grid = (bsz, seq // ts)
    full = lambda a: pl.BlockSpec(a.shape, lambda b, j: (0,) * a.ndim)
    ws = [lw[k] for k in ('w_out', 'g_memx', 'm_w_q', 'g_mq', 'm_w_o', 'g_ffn', 'r_hi', 'r_lo', 'r_b')]
    tok = lambda w: pl.BlockSpec((1, ts, w), lambda b, j: (b, j, 0))
    return pl.pallas_call(
        _post_attn_kernel, grid=grid,
        in_specs=[tok(d_model),
                  pl.BlockSpec((1, A_HEADS, A_HEAD_DIM, ts), lambda b, j: (b, 0, 0, j)),
                  pl.BlockSpec((1, B_HEADS, B_V, ts), lambda b, j: (b, 0, 0, j)),
                  pl.BlockSpec((1, n_mem, hw), lambda b, j: (b, 0, 0)),
                  pl.BlockSpec((1, n_mem, hw), lambda b, j: (b, 0, 0))] + [full(w) for w in ws],
        out_specs=[tok(d_model), tok(d_model // 2), tok(LANES), pl.BlockSpec((1, LANES), lambda b, j: (0, 0))],
        out_shape=[jax.ShapeDtypeStruct((bsz, seq, d_model), F32),
                   jax.ShapeDtypeStruct((bsz, seq, d_model // 2), jnp.uint32),
                   jax.ShapeDtypeStruct((bsz, seq, LANES), F32),
                   jax.ShapeDtypeStruct((1, LANES), F32)],
        scratch_shapes=[pltpu.VMEM((1, LANES), F32)],
        compiler_params=_cparams(("arbitrary", "arbitrary")), name="post_attn",
    )(x, oa, ob, mk, mv, *ws)


def _row_copy(src, dst, sem, s_row, d_row):
    return pltpu.make_async_copy(src.at[pl.ds(s_row, 1)], dst.at[pl.ds(d_row, 1)], sem)


def _dispatch_kernel(dest_ref, h_ref, xs_in, xs_out, sem, *, td):
    del xs_in

    def issue(t, c):
        for kk in range(TOP_K):
            _row_copy(h_ref, xs_out, sem, t, dest_ref[0, 0, TOP_K * t + kk]).start()
        return c

    lax.fori_loop(0, td, issue, 0)

    def drain(t, c):
        for kk in range(TOP_K):
            _row_copy(h_ref, xs_out, sem, 0, 0).wait()
        return c

    lax.fori_loop(0, td, drain, 0)


def _dispatch(h_packed, dest, n_rows, td):
    n_tok, width = h_packed.shape
    dest3 = dest.reshape(n_tok // td, 1, TOP_K * td)
    xs0 = jnp.zeros((n_rows, width), h_packed.dtype)
    return pl.pallas_call(
        functools.partial(_dispatch_kernel, td=td), grid=(n_tok // td,),
        in_specs=[pl.BlockSpec((1, 1, TOP_K * td), lambda i: (i, 0, 0), memory_space=pltpu.SMEM),
                  pl.BlockSpec((td, width), lambda i: (i, 0)), pl.BlockSpec(memory_space=pl.ANY)],
        out_specs=pl.BlockSpec(memory_space=pl.ANY),
        out_shape=jax.ShapeDtypeStruct((n_rows, width), h_packed.dtype),
        scratch_shapes=[pltpu.SemaphoreType.DMA(())],
        input_output_aliases={2: 0},
        compiler_params=_cparams(("arbitrary",)), name="moe_dispatch",
    )(dest3, h_packed, xs0)


def _experts_kernel(be_ref, nu_ref, xs_ref, w_gu, w_dn, ys_ref):
    i = pl.program_id(0)

    @pl.when(i < nu_ref[0])
    def _():
        packed = xs_ref[...]
        lo = lax.bitcast_convert_type(packed << 16, F32)
        hi = lax.bitcast_convert_type(packed & jnp.uint32(0xFFFF0000), F32)
        xb = jnp.concatenate([lo, hi], axis=-1).astype(BF16)
        gu = _nn(xb, w_gu[0])
        d_e = gu.shape[1] // 2
        g = gu[:, :d_e]
        a = g * (1.0 / (1.0 + jnp.exp(-g))) * gu[:, d_e:]
        ys_ref[...] = _nn(a.astype(BF16), w_dn[0])

    @pl.when(i >= nu_ref[0])
    def _():
        ys_ref[...] = jnp.zeros_like(ys_ref)


def _experts(xs, block_expert, n_used, lw, bm):
    n_rows, width = xs.shape
    d_model = 2 * width
    w_gu, w_dn = lw['w_gu'], lw['w_dn']
    grid_spec = pltpu.PrefetchScalarGridSpec(
        num_scalar_prefetch=2, grid=(n_rows // bm,),
        in_specs=[pl.BlockSpec((bm, width), lambda i, be, nu: (i, 0)),
                  pl.BlockSpec((1,) + w_gu.shape[1:], lambda i, be, nu: (be[i], 0, 0)),
                  pl.BlockSpec((1,) + w_dn.shape[1:], lambda i, be, nu: (be[i], 0, 0))],
        out_specs=pl.BlockSpec((bm, d_model), lambda i, be, nu: (i, 0)))
    return pl.pallas_call(
        _experts_kernel, grid_spec=grid_spec,
        out_shape=jax.ShapeDtypeStruct((n_rows, d_model), F32),
        compiler_params=_cparams(("arbitrary",)), name="moe_experts",
    )(block_expert, n_used, xs, w_gu, w_dn)


def _combine_kernel(dest_ref, ys_ref, x_ref, route_ref, o_ref, buf, sem, *, tc):
    def issue(t, c):
        for kk in range(TOP_K):
            pltpu.make_async_copy(ys_ref.at[pl.ds(dest_ref[0, 0, TOP_K * t + kk], 1)],
                                  buf.at[kk, pl.ds(t, 1)], sem).start()
        return c

    lax.fori_loop(0, tc, issue, 0)

    def drain(t, c):
        for kk in range(TOP_K):
            pltpu.make_async_copy(ys_ref.at[pl.ds(0, 1)], buf.at[kk, pl.ds(0, 1)], sem).wait()
        return c

    lax.fori_loop(0, tc, drain, 0)
    route = route_ref[...]
    o_ref[...] = x_ref[...] + route[:, 2:3] * buf[0] + route[:, 3:4] * buf[1]


def _combine(ys, dest, x, route, tc):
    n_tok, d_model = x.shape
    dest3 = dest.reshape(n_tok // tc, 1, TOP_K * tc)
    return pl.pallas_call(
        functools.partial(_combine_kernel, tc=tc), grid=(n_tok // tc,),
        in_specs=[pl.BlockSpec((1, 1, TOP_K * tc), lambda i: (i, 0, 0), memory_space=pltpu.SMEM),
                  pl.BlockSpec(memory_space=pl.ANY),
                  pl.BlockSpec((tc, d_model), lambda i: (i, 0)),
                  pl.BlockSpec((tc, LANES), lambda i: (i, 0))],
        out_specs=pl.BlockSpec((tc, d_model), lambda i: (i, 0)),
        out_shape=jax.ShapeDtypeStruct((n_tok, d_model), F32),
        scratch_shapes=[pltpu.VMEM((TOP_K, tc, d_model), F32), pltpu.SemaphoreType.DMA(())],
        compiler_params=_cparams(("arbitrary",)), name="moe_combine",
    )(dest3, ys, x, route)


def _moe(x2, h_packed, route, counts, lw, cfg):
    bsz, seq, d_model = x2.shape
    n_tok = bsz * seq
    bm = cfg['bm']
    cnt = counts[0, :N_EXPERTS].astype(jnp.int32)
    padded = ((cnt + bm - 1) // bm) * bm
    ends = jnp.cumsum(padded)
    starts = ends - padded
    n_blk = -(-(n_tok * TOP_K + N_EXPERTS * (bm - 1)) // bm)
    route2 = route.reshape(n_tok, LANES)
    expert = route2[:, 0:TOP_K].astype(jnp.int32)
    rank = route2[:, 4:4 + TOP_K].astype(jnp.int32)
    dest = (starts[expert] + rank).reshape(-1)
    block_expert = jnp.clip(
        jnp.searchsorted(ends, jnp.arange(n_blk, dtype=jnp.int32) * bm, side='right'), 0, N_EXPERTS - 1
    ).astype(jnp.int32)
    n_used = (ends[-1:] // bm).astype(jnp.int32)
    xs = _dispatch(h_packed.reshape(n_tok, d_model // 2), dest, n_blk * bm, cfg['td'])
    ys = _experts(xs, block_expert, n_used, lw, bm)
    out = _combine(ys, dest, x2.reshape(n_tok, d_model), route2, cfg['tc'])
    return out.reshape(bsz, seq, d_model)


def _config(seq):
    if seq % 2048 == 0:
        return dict(ts=512, tq_a=512, tq_b=2048, tk=256, wp=512, bm=256, td=256, tc=256)
    return dict(ts=128, tq_a=128, tq_b=256, tk=64, wp=128, bm=64, td=64, tc=64)


def _trunk(x, mem, layers, cfg):
    rt = _rope_tables(x.shape[1])
    for lw in layers:
        qa, ka, va, qb, kb, vb = _in_proj(x, lw, rt, cfg['ts'])
        att = dict(tk=cfg['tk'], wp=cfg['wp'])
        oa = _attention(qa, ka, va, dq=A_HEAD_DIM, tqb=cfg['tq_a'], name="attn_a", **att)
        ob = _attention(qb, kb, vb, dq=LANES, tqb=cfg['tq_b'], name="attn_b", **att)
        mk, mv = _mem_kv(mem, lw)
        x2, h_packed, route, counts = _post_attn(x, oa, ob, mk, mv, lw, cfg['ts'])
        x = _moe(x2, h_packed, route, counts, lw, cfg)
    return x


def kernel(x_prompt, x_sample, mem_prompt, mem_sample, norm_mix, w_in, a_q_norm, a_k_norm, b_cq_norm, b_ckv_norm, b_w_uq, b_w_ukv, b_q_norm, b_k_norm, w_out, norm_mem_x, norm_mem_m, m_w_q, m_w_kv, m_q_norm, m_k_norm, m_w_o, norm_ffn, r_coarse_w, r_coarse_b, r_fine_w, r_fine_b, e_w_gate, e_w_up, e_w_down):
    p = dict(norm_mix=norm_mix, w_in=w_in, a_q_norm=a_q_norm, a_k_norm=a_k_norm, b_cq_norm=b_cq_norm,
             b_ckv_norm=b_ckv_norm, b_w_uq=b_w_uq, b_w_ukv=b_w_ukv, b_q_norm=b_q_norm, b_k_norm=b_k_norm,
             w_out=w_out, norm_mem_x=norm_mem_x, norm_mem_m=norm_mem_m, m_w_q=m_w_q, m_w_kv=m_w_kv,
             m_q_norm=m_q_norm, m_k_norm=m_k_norm, m_w_o=m_w_o, norm_ffn=norm_ffn, r_coarse_w=r_coarse_w,
             r_coarse_b=r_coarse_b, r_fine_w=r_fine_w, r_fine_b=r_fine_b, e_w_gate=e_w_gate, e_w_up=e_w_up,
             e_w_down=e_w_down)
    layers = [_prep_layer(p, i) for i in range(w_in.shape[0])]
    y_prompt = _trunk(x_prompt, mem_prompt, layers, _config(x_prompt.shape[1]))
    y_sample = _trunk(x_sample, mem_sample, layers, _config(x_sample.shape[1]))
    return (y_prompt, y_sample)
```

```python
import functools

import numpy as np
import jax
import jax.numpy as jnp
from jax import lax
from jax.experimental import pallas as pl
from jax.experimental.pallas import tpu as pltpu

GRID_W = 64
ROPE_THETA = 10000.0
EPS = 1e-6
LOG2E = 1.4426950408889634

A_HEADS, A_KV_HEADS, A_HEAD_DIM = 8, 2, 64
B_HEADS, B_NOPE, B_ROPE, B_V = 8, 64, 32, 64
B_QK = B_NOPE + B_ROPE
B_Q_LORA, B_KV_LORA = 256, 128
M_HEADS, M_HEAD_DIM = 4, 128
N_GROUPS, EXPERTS_PER_GROUP, TOP_K = 4, 8, 2
N_EXPERTS = N_GROUPS * EXPERTS_PER_GROUP
LANES = 128
SUM_ROWS = 16
ISSUE_UNROLL = 8

VMEM_LIMIT = 56 * 1024 * 1024

BF16 = jnp.bfloat16
F32 = jnp.float32


def _cparams(sem, flags=None):
    return pltpu.CompilerParams(dimension_semantics=sem, vmem_limit_bytes=VMEM_LIMIT, flags=flags)


def _nt(a, b):
    return lax.dot_general(a, b, (((1,), (1,)), ((), ())), preferred_element_type=F32)


def _tn(a, b):
    return lax.dot_general(a, b, (((0,), (0,)), ((), ())), preferred_element_type=F32)


def _nn(a, b):
    return jnp.dot(a, b, preferred_element_type=F32)


def _rope_angles(seq_len, rot_dim):
    rows = seq_len // GRID_W
    r = jnp.repeat(jnp.arange(rows, dtype=F32), GRID_W)
    c = jnp.tile(jnp.arange(GRID_W, dtype=F32), rows)
    axis_dim = rot_dim // 2
    inv = ROPE_THETA ** (-(jnp.arange(axis_dim // 2, dtype=F32) * 2.0) / axis_dim)
    return jnp.concatenate([r[:, None] * inv, c[:, None] * inv], axis=-1)


def _rope_tables(seq_len):
    ang_a = _rope_angles(seq_len, A_HEAD_DIM)
    ang_b = _rope_angles(seq_len, B_ROPE)
    ca, sa = jnp.cos(ang_a), jnp.sin(ang_a)
    cb, sb = jnp.cos(ang_b), jnp.sin(ang_b)
    z64 = jnp.zeros((seq_len, 64), F32)
    z32 = jnp.zeros((seq_len, 32), F32)
    one64 = jnp.ones((seq_len, 64), F32)
    return dict(
        cos_a_t=ca.T, sin_a_t=sa.T,
        cos_b_t=cb.T, sin_b_t=sb.T,
        cos_ka=jnp.concatenate([ca, ca, z64], -1),
        sin_ka=jnp.concatenate([-sa, sa, z64], -1),
        cos_kb=jnp.concatenate([one64, cb, cb, z32], -1),
        sin_kb=jnp.concatenate([z64, -sb, sb, z32], -1),
    )


_EVEN64, _ODD64 = np.arange(0, 64, 2), np.arange(1, 64, 2)
_PERM64 = np.concatenate([_EVEN64, _ODD64])
_SWAP64 = np.concatenate([_ODD64, _EVEN64])
_EVEN32, _ODD32 = np.arange(0, 32, 2), np.arange(1, 32, 2)
_PERM32 = np.concatenate([_EVEN32, _ODD32])
_SWAP32 = np.concatenate([_ODD32, _EVEN32])


def _prep_layer(p, i):
    d_model = p['w_in'].shape[1]
    w_in = p['w_in'][i]
    o_ak = A_HEADS * A_HEAD_DIM
    o_av = o_ak + A_KV_HEADS * A_HEAD_DIM
    o_cq = o_av + A_KV_HEADS * A_HEAD_DIM
    o_ckv = o_cq + B_Q_LORA
    o_kr = o_ckv + B_KV_LORA
    zc64 = jnp.zeros((d_model, 64), F32)
    zc32 = jnp.zeros((d_model, 32), F32)

    aq_cols = np.concatenate([h * A_HEAD_DIM + _PERM64 for h in range(A_HEADS)])
    w_d = jnp.concatenate([w_in[:, aq_cols], w_in[:, o_av:o_cq], w_in[:, o_cq:o_ckv], w_in[:, o_ckv:o_kr]], -1)
    ak, akp = [], []
    for j in range(A_KV_HEADS):
        ak += [w_in[:, o_ak + j * A_HEAD_DIM + _PERM64], zc64]
        akp += [w_in[:, o_ak + j * A_HEAD_DIM + _SWAP64], zc64]
    kr = [zc64, w_in[:, o_kr + _PERM32], zc32]
    krp = [zc64, w_in[:, o_kr + _SWAP32], zc32]
    w_t = jnp.concatenate(ak + akp + [w_in[:, o_ckv:o_kr]] + kr + krp, -1)

    z64 = jnp.zeros((64,), F32)
    z32 = jnp.zeros((32,), F32)
    a_k = p['a_k_norm'][i]
    b_q = p['b_q_norm'][i]
    b_k = p['b_k_norm'][i]
    w_uq = p['b_w_uq'][i]
    uq_cols = np.concatenate([np.concatenate([h * B_QK + np.arange(B_NOPE), h * B_QK + B_NOPE + _PERM32])
                              for h in range(B_HEADS)])
    w_ukv = p['b_w_ukv'][i]
    zk = jnp.zeros((B_KV_LORA, 64), F32)
    wk_pad = jnp.concatenate(
        sum([[w_ukv[:, h * (B_NOPE + B_V):h * (B_NOPE + B_V) + B_NOPE], zk] for h in range(B_HEADS)], []), -1)
    wv = jnp.concatenate(
        [w_ukv[:, h * (B_NOPE + B_V) + B_NOPE:(h + 1) * (B_NOPE + B_V)] for h in range(B_HEADS)], -1)

    r_w = jnp.concatenate([p['r_coarse_w'][i], jnp.zeros((d_model, LANES - N_GROUPS), F32),
                           p['r_fine_w'][i], jnp.zeros((d_model, LANES - N_EXPERTS), F32)], -1)
    r_hi = r_w.astype(BF16)
    r_lo = (r_w - r_hi.astype(F32)).astype(BF16)
    r_b = jnp.concatenate([p['r_coarse_b'][i], jnp.zeros((LANES - N_GROUPS,), F32),
                           p['r_fine_b'][i], jnp.zeros((LANES - N_EXPERTS,), F32)])[None, :]
    return dict(
        g_mix=p['norm_mix'][i][None, :],
        w_d_t=w_d.T.astype(BF16),
        w_t=w_t.astype(BF16),
        g_aq=p['a_q_norm'][i][_PERM64][:, None],
        g_ak=jnp.concatenate([a_k[_PERM64], z64])[None, :],
        g_akp=jnp.concatenate([a_k[_SWAP64], z64])[None, :],
        g_cq=p['b_cq_norm'][i][:, None],
        g_ckv_row=p['b_ckv_norm'][i][None, :],
        g_ckv_col=p['b_ckv_norm'][i][:, None],
        w_uq_t=w_uq[:, uq_cols].T.astype(BF16),
        g_bq=jnp.concatenate([b_q[:B_NOPE], b_q[B_NOPE + _PERM32]])[:, None],
        wk_pad=wk_pad.astype(BF16),
        wv_t=wv.T.astype(BF16),
        g_bk=jnp.concatenate([b_k[:B_NOPE], b_k[B_NOPE + _PERM32], z32])[None, :],
        g_bkp=jnp.concatenate([z64, b_k[B_NOPE + _SWAP32], z32])[None, :],
        w_out=p['w_out'][i].astype(BF16),
        g_memx=p['norm_mem_x'][i][None, :],
        g_memm=p['norm_mem_m'][i][None, :],
        m_w_q=p['m_w_q'][i].astype(BF16),
        m_w_kv=p['m_w_kv'][i].astype(BF16),
        g_mq=p['m_q_norm'][i][None, :],
        g_mk=p['m_k_norm'][i][None, :],
        m_w_o=p['m_w_o'][i].astype(BF16),
        g_ffn=p['norm_ffn'][i][None, :],
        r_hi=r_hi, r_lo=r_lo, r_b=r_b,
        w_gu=jnp.concatenate([p['e_w_gate'][i], p['e_w_up'][i]], -1).astype(BF16),
        w_dn=p['e_w_down'][i].astype(BF16),
    )


def _in_proj_kernel(x_ref, g_mix, w_d_t, w_t, g_aq, g_ak, g_akp, g_cq, g_ckv_row, g_ckv_col, w_uq_t, g_bq,
                    wk_pad, wv_t, g_bk, g_bkp, cos_a_t, sin_a_t, cos_b_t, sin_b_t, cos_ka, sin_ka, cos_kb, sin_kb,
                    qa_ref, ka_ref, va_ref, qb_ref, kb_ref, vb_ref):
    x = x_ref[0]
    h = (x * lax.rsqrt(jnp.mean(x * x, axis=-1, keepdims=True) + EPS) * g_mix[...]).astype(BF16)
    pd = _nt(w_d_t[...], h)
    pt = _nn(h, w_t[...])
    ts = x.shape[0]

    ca, sa = cos_a_t[...], sin_a_t[...]
    qscale = A_HEAD_DIM ** -0.5 * LOG2E
    for hd in range(A_HEADS):
        q = pd[hd * 64:(hd + 1) * 64]
        qn = q * lax.rsqrt(jnp.mean(q * q, axis=0, keepdims=True) + EPS) * g_aq[...]
        x1, x2 = qn[:32], qn[32:]
        rot = jnp.concatenate([x1 * ca - x2 * sa, x1 * sa + x2 * ca], axis=0) * qscale
        qa_ref[0, hd] = rot.astype(BF16)
    o_av = A_HEADS * A_HEAD_DIM
    for j in range(A_KV_HEADS):
        va_ref[0, j] = pd[o_av + j * 64:o_av + (j + 1) * 64].astype(BF16)
    for j in range(A_KV_HEADS):
        k = pt[:, j * LANES:(j + 1) * LANES]
        kp = pt[:, (A_KV_HEADS + j) * LANES:(A_KV_HEADS + j + 1) * LANES]
        rs = lax.rsqrt(jnp.sum(k * k, axis=-1, keepdims=True) * (1.0 / A_HEAD_DIM) + EPS)
        ka_ref[0, j] = (rs * (k * g_ak[...] * cos_ka[...] + kp * g_akp[...] * sin_ka[...])).astype(BF16)

    o_cq = o_av + A_KV_HEADS * A_HEAD_DIM
    cq = pd[o_cq:o_cq + B_Q_LORA]
    cqn = (cq * lax.rsqrt(jnp.mean(cq * cq, axis=0, keepdims=True) + EPS) * g_cq[...]).astype(BF16)
    bq = _nn(w_uq_t[...], cqn)
    cb, sb = cos_b_t[...], sin_b_t[...]
    bscale = B_QK ** -0.5 * LOG2E
    zpad = jnp.zeros((LANES - B_QK, ts), F32)
    for hd in range(B_HEADS):
        q = bq[hd * B_QK:(hd + 1) * B_QK]
        qn = q * lax.rsqrt(jnp.mean(q * q, axis=0, keepdims=True) + EPS) * g_bq[...]
        x1, x2 = qn[B_NOPE:B_NOPE + 16], qn[B_NOPE + 16:]
        rot = jnp.concatenate([qn[:B_NOPE], x1 * cb - x2 * sb, x1 * sb + x2 * cb], axis=0) * bscale
        qb_ref[0, hd] = jnp.concatenate([rot, zpad], axis=0).astype(BF16)

    o_ckv_t = 2 * A_KV_HEADS * LANES
    ckv = pt[:, o_ckv_t:o_ckv_t + B_KV_LORA]
    ckvn = (ckv * lax.rsqrt(jnp.mean(ckv * ckv, axis=-1, keepdims=True) + EPS) * g_ckv_row[...]).astype(BF16)
    k_nope = _nn(ckvn, wk_pad[...])
    kr = pt[:, o_ckv_t + LANES:o_ckv_t + 2 * LANES]
    krp = pt[:, o_ckv_t + 2 * LANES:o_ckv_t + 3 * LANES]
    rope_part = krp * g_bkp[...] * sin_kb[...]
    for hd in range(B_HEADS):
        k = k_nope[:, hd * LANES:(hd + 1) * LANES] + kr
        rs = lax.rsqrt(jnp.sum(k * k, axis=-1, keepdims=True) * (1.0 / B_QK) + EPS)
        kb_ref[0, hd] = (rs * (k * g_bk[...] * cos_kb[...] + rope_part)).astype(BF16)
    o_ckv = o_cq + B_Q_LORA
    ckv_t = pd[o_ckv:o_ckv + B_KV_LORA]
    ckvn_t = (ckv_t * lax.rsqrt(jnp.mean(ckv_t * ckv_t, axis=0, keepdims=True) + EPS) * g_ckv_col[...]).astype(BF16)
    bv = _nn(wv_t[...], ckvn_t)
    for hd in range(B_HEADS):
        vb_ref[0, hd] = bv[hd * B_V:(hd + 1) * B_V].astype(BF16)


def _in_proj(x, lw, rt, ts):
    bsz, seq, d_model = x.shape
    grid = (bsz, seq // ts)
    full = lambda a: pl.BlockSpec(a.shape, lambda b, j: (0,) * a.ndim)
    weights = [lw[k] for k in ('g_mix', 'w_d_t', 'w_t', 'g_aq', 'g_ak', 'g_akp', 'g_cq', 'g_ckv_row', 'g_ckv_col',
                               'w_uq_t', 'g_bq', 'wk_pad', 'wv_t', 'g_bk', 'g_bkp')]
    tabs_t = [rt[k] for k in ('cos_a_t', 'sin_a_t', 'cos_b_t', 'sin_b_t')]
    tabs_k = [rt[k] for k in ('cos_ka', 'sin_ka', 'cos_kb', 'sin_kb')]
    in_specs = ([pl.BlockSpec((1, ts, d_model), lambda b, j: (b, j, 0))] + [full(w) for w in weights]
                + [pl.BlockSpec((t.shape[0], ts), lambda b, j: (0, j)) for t in tabs_t]
                + [pl.BlockSpec((ts, LANES), lambda b, j: (j, 0)) for _ in tabs_k])
    t_major = lambda nh: pl.BlockSpec((1, nh, ts, LANES), lambda b, j: (b, 0, j, 0))
    d_major = lambda nh, d: pl.BlockSpec((1, nh, d, ts), lambda b, j: (b, 0, 0, j))
    out_shape = [
        jax.ShapeDtypeStruct((bsz, A_HEADS, A_HEAD_DIM, seq), BF16),
        jax.ShapeDtypeStruct((bsz, A_KV_HEADS, seq, LANES), BF16),
        jax.ShapeDtypeStruct((bsz, A_KV_HEADS, A_HEAD_DIM, seq), BF16),
        jax.ShapeDtypeStruct((bsz, B_HEADS, LANES, seq), BF16),
        jax.ShapeDtypeStruct((bsz, B_HEADS, seq, LANES), BF16),
        jax.ShapeDtypeStruct((bsz, B_HEADS, B_V, seq), BF16),
    ]
    out_specs = [d_major(A_HEADS, A_HEAD_DIM), t_major(A_KV_HEADS), d_major(A_KV_HEADS, A_HEAD_DIM),
                 d_major(B_HEADS, LANES), t_major(B_HEADS), d_major(B_HEADS, B_V)]
    return pl.pallas_call(
        _in_proj_kernel, grid=grid, in_specs=in_specs, out_specs=out_specs, out_shape=out_shape,
        compiler_params=_cparams(("parallel", "parallel")), name="in_proj",
    )(x, *weights, *tabs_t, *tabs_k)


def _attn_kernel(q_ref, k_ref, v_ref, o_ref, s_scr, cmax_scr, *, dq, tk, wp, interleave):
    n_heads, _, tqb = q_ref.shape[1:]
    seq = k_ref.shape[2]
    dv = v_ref.shape[2]
    panels = [(g, pnl * wp) for g in range(n_heads) for pnl in range(tqb // wp)]
    n_chunks = seq // tk

    ones_rows = jnp.ones((SUM_ROWS, tk), BF16)

    def k_chunk(c):
        return k_ref[0, 0, pl.ds(pl.multiple_of(c * tk, tk), tk), :][:, :dq]

    def v_chunk(c):
        return jnp.concatenate([v_ref[0, 0, :, pl.ds(pl.multiple_of(c * tk, tk), tk)], ones_rows], axis=0)

    def score_panel(k, slot, n):
        g, lo = panels[n]
        s = _nn(k, q_ref[0, g, :, lo:lo + wp])
        s_scr[slot, n] = s
        cmax_scr[slot, n] = jnp.max(s, axis=0, keepdims=True)

    def update_panel(v, slot, n, m, acc):
        m_new = jnp.maximum(m, cmax_scr[slot, n])
        p = jnp.exp2((s_scr[slot, n] - m_new).astype(BF16))
        return m_new, jnp.exp2(m - m_new) * acc + _nn(v, p)

    def step(c_next, c_cur, slot_cur, state):
        k = None if c_next is None else k_chunk(c_next)
        v = v_chunk(c_cur)
        if k is not None and not interleave:
            for n in range(len(panels)):
                score_panel(k, 1 - slot_cur, n)
        out = []
        for n, (m, acc) in enumerate(state):
            if k is not None and interleave:
                score_panel(k, 1 - slot_cur, n)
            out.append(update_panel(v, slot_cur, n, m, acc))
        return tuple(out)

    def body(j, state):
        c = 2 * j
        state = step(c + 1, c, 0, state)
        return step(c + 2, c + 1, 1, state)

    state = tuple((jnp.full((1, wp), -jnp.inf, F32), jnp.zeros((dv + SUM_ROWS, wp), F32)) for _ in panels)
    k0 = k_chunk(0)
    for n in range(len(panels)):
        score_panel(k0, 0, n)
    state = lax.fori_loop(0, n_chunks // 2 - 1, body, state)
    state = step(n_chunks - 1, n_chunks - 2, 0, state)
    state = step(None, n_chunks - 1, 1, state)
    for (g, lo), (_, acc) in zip(panels, state):
        o_ref[0, g, :, lo:lo + wp] = (acc[:dv] * (1.0 / acc[dv:dv + 1])).astype(BF16)


def _attention(q_t, k, v_t, *, dq, tqb, tk, wp, interleave, name):
    bsz, hq, dq_rows, seq = q_t.shape
    hk = k.shape[1]
    grp = hq // hk
    dv = v_t.shape[2]
    grid = (bsz, hk, seq // tqb)
    return pl.pallas_call(
        functools.partial(_attn_kernel, dq=dq, tk=tk, wp=wp, interleave=interleave),
        grid=grid,
        in_specs=[pl.BlockSpec((1, grp, dq_rows, tqb), lambda b, j, i: (b, j, 0, i)),
                  pl.BlockSpec((1, 1, seq, LANES), lambda b, j, i: (b, j, 0, 0)),
                  pl.BlockSpec((1, 1, dv, seq), lambda b, j, i: (b, j, 0, 0))],
        out_specs=pl.BlockSpec((1, grp, dv, tqb), lambda b, j, i: (b, j, 0, i)),
        out_shape=jax.ShapeDtypeStruct((bsz, hq, dv, seq), BF16),
        scratch_shapes=[pltpu.VMEM((2, grp * (tqb // wp), tk, wp), F32),
                        pltpu.VMEM((2, grp * (tqb // wp), 1, wp), F32)],
        compiler_params=_cparams(("parallel", "parallel", "parallel")), name=name,
    )(q_t, k, v_t)


def _mem_kv_kernel(mem_ref, g_memm, w_kv, g_mk, k_ref, v_ref):
    m = mem_ref[0]
    mn = (m * lax.rsqrt(jnp.mean(m * m, axis=-1, keepdims=True) + EPS) * g_memm[...]).astype(BF16)
    kv = _nn(mn, w_kv[...])
    hw = M_HEADS * M_HEAD_DIM
    for hh in range(M_HEADS):
        k = kv[:, hh * M_HEAD_DIM:(hh + 1) * M_HEAD_DIM]
        kn = k * lax.rsqrt(jnp.mean(k * k, axis=-1, keepdims=True) + EPS) * g_mk[...]
        k_ref[0, :, hh * M_HEAD_DIM:(hh + 1) * M_HEAD_DIM] = kn.astype(BF16)
    v_ref[0] = kv[:, hw:].astype(BF16)


def _mem_kv(mem, lw):
    bsz, n_mem, d_model = mem.shape
    hw = M_HEADS * M_HEAD_DIM
    full = lambda a: pl.BlockSpec(a.shape, lambda b: (0,) * a.ndim)
    ws = [lw['g_memm'], lw['m_w_kv'], lw['g_mk']]
    return pl.pallas_call(
        _mem_kv_kernel, grid=(bsz,),
        in_specs=[pl.BlockSpec((1, n_mem, d_model), lambda b: (b, 0, 0))] + [full(w) for w in ws],
        out_specs=[pl.BlockSpec((1, n_mem, hw), lambda b: (b, 0, 0))] * 2,
        out_shape=[jax.ShapeDtypeStruct((bsz, n_mem, hw), BF16)] * 2,
        compiler_params=_cparams(("parallel",)), name="mem_kv",
    )(mem, *ws)


def _pack_bf16_pairs(lo, hi):
    lo_b = lax.bitcast_convert_type(lo.astype(BF16).astype(F32), jnp.uint32)
    hi_b = lax.bitcast_convert_type(hi.astype(BF16).astype(F32), jnp.uint32)
    return (hi_b & jnp.uint32(0xFFFF0000)) | (lo_b >> 16)


def _post_attn_kernel(x_ref, oa_ref, ob_ref, mk_ref, mv_ref, w_out, g_memx, m_w_q, g_mq, m_w_o, g_ffn,
                      r_hi, r_lo, r_b, x_out, h_out, route_out, cnt_out, cnt_scr):
    first = jnp.logical_and(pl.program_id(0) == 0, pl.program_id(1) == 0)

    @pl.when(first)
    def _():
        cnt_scr[...] = jnp.zeros_like(cnt_scr)

    ts = x_ref.shape[1]
    half = A_HEADS * A_HEAD_DIM
    oa = oa_ref[0].reshape(half, ts)
    ob = ob_ref[0].reshape(half, ts)
    x1 = x_ref[0] + _tn(oa, w_out[:half, :]) + _tn(ob, w_out[half:, :])

    h2 = (x1 * lax.rsqrt(jnp.mean(x1 * x1, axis=-1, keepdims=True) + EPS) * g_memx[...]).astype(BF16)
    qm = _nn(h2, m_w_q[...])
    mscale = M_HEAD_DIM ** -0.5 * LOG2E
    heads = []
    for hh in range(M_HEADS):
        sl = slice(hh * M_HEAD_DIM, (hh + 1) * M_HEAD_DIM)
        q = qm[:, sl]
        qn = (q * lax.rsqrt(jnp.mean(q * q, axis=-1, keepdims=True) + EPS) * g_mq[...] * mscale).astype(BF16)
        s = _nt(qn, mk_ref[0, :, sl])
        p = jnp.exp2(s - jnp.max(s, axis=-1, keepdims=True))
        o = _nn(p.astype(BF16), mv_ref[0, :, sl]) * (1.0 / jnp.sum(p, axis=-1, keepdims=True))
        heads.append(o.astype(BF16))
    x2 = x1 + _nn(jnp.concatenate(heads, axis=-1), m_w_o[...])
    x_out[0] = x2

    h3 = x2 * lax.rsqrt(jnp.mean(x2 * x2, axis=-1, keepdims=True) + EPS) * g_ffn[...]
    d_half = h3.shape[1] // 2
    h_out[0] = _pack_bf16_pairs(h3[:, :d_half], h3[:, d_half:])

    h_hi = h3.astype(BF16)
    h_lo = (h3 - h_hi.astype(F32)).astype(BF16)
    logits = _nn(h_hi, r_hi[...]) + _nn(h_hi, r_lo[...]) + _nn(h_lo, r_hi[...]) + r_b[...]
    lane = lax.broadcasted_iota(jnp.int32, (ts, LANES), 1)
    neg = jnp.float32(-jnp.inf)
    lc = jnp.where(lane < N_GROUPS, logits[:, :LANES], neg)
    mc = jnp.max(lc, axis=-1, keepdims=True)
    g_idx = jnp.min(jnp.where(lc == mc, lane, LANES), axis=-1, keepdims=True)
    g_w = 1.0 / jnp.sum(jnp.exp(lc - mc), axis=-1, keepdims=True)
    lf = jnp.where((lane >> 3) == g_idx, logits[:, LANES:], neg)
    m1 = jnp.max(lf, axis=-1, keepdims=True)
    i1 = jnp.min(jnp.where(lf == m1, lane, LANES), axis=-1, keepdims=True)
    lf2 = jnp.where(lane == i1, neg, lf)
    m2 = jnp.max(lf2, axis=-1, keepdims=True)
    i2 = jnp.min(jnp.where(lf2 == m2, lane, LANES), axis=-1, keepdims=True)
    r21 = jnp.exp(m2 - m1)
    gate1 = g_w / (1.0 + r21)
    gate2 = g_w * r21 / (1.0 + r21)

    hot1 = lane == i1
    hot2 = lane == i2
    onehot = jnp.where(jnp.logical_or(hot1, hot2), 1.0, 0.0)
    row = lax.broadcasted_iota(jnp.int32, (ts, ts), 0)
    col = lax.broadcasted_iota(jnp.int32, (ts, ts), 1)
    tri = jnp.where(col < row, 1.0, 0.0).astype(BF16)
    base = _nn(tri, onehot.astype(BF16)) + cnt_scr[...]
    rank1 = jnp.sum(jnp.where(hot1, base, 0.0), axis=-1, keepdims=True)
    rank2 = jnp.sum(jnp.where(hot2, base, 0.0), axis=-1, keepdims=True)
    cnt_scr[...] += jnp.sum(onehot, axis=0, keepdims=True)
    cnt_out[...] = cnt_scr[...]

    vals = (i1.astype(F32), i2.astype(F32), gate1, gate2, rank1, rank2)
    route = jnp.zeros((ts, LANES), F32)
    for n, v in enumerate(vals):
        route = jnp.where(lane == n, v, route)
    route_out[0] = route


def _post_attn(x, oa, ob, mk, mv, lw, ts):
    bsz, seq, d_model = x.shape
    n_mem, hw = mk.shape[1:]
    grid = (bsz, seq // ts)
    full = lambda a: pl.BlockSpec(a.shape, lambda b, j: (0,) * a.ndim)
    ws = [lw[k] for k in ('w_out', 'g_memx', 'm_w_q', 'g_mq', 'm_w_o', 'g_ffn', 'r_hi', 'r_lo', 'r_b')]
    tok = lambda w: pl.BlockSpec((1, ts, w), lambda b, j: (b, j, 0))
    return pl.pallas_call(
        _post_attn_kernel, grid=grid,
        in_specs=[tok(d_model),
                  pl.BlockSpec((1, A_HEADS, A_HEAD_DIM, ts), lambda b, j: (b, 0, 0, j)),
                  pl.BlockSpec((1, B_HEADS, B_V, ts), lambda b, j: (b, 0, 0, j)),
                  pl.BlockSpec((1, n_mem, hw), lambda b, j: (b, 0, 0)),
                  pl.BlockSpec((1, n_mem, hw), lambda b, j: (b, 0, 0))] + [full(w) for w in ws],
        out_specs=[tok(d_model), tok(d_model // 2), tok(LANES), pl.BlockSpec((1, LANES), lambda b, j: (0, 0))],
        out_shape=[jax.ShapeDtypeStruct((bsz, seq, d_model), F32),
                   jax.ShapeDtypeStruct((bsz, seq, d_model // 2), jnp.uint32),
                   jax.ShapeDtypeStruct((bsz, seq, LANES), F32),
                   jax.ShapeDtypeStruct((1, LANES), F32)],
        scratch_shapes=[pltpu.VMEM((1, LANES), F32)],
        compiler_params=_cparams(("arbitrary", "arbitrary")), name="post_attn",
    )(x, oa, ob, mk, mv, *ws)


def _row_copy(src, dst, sem, s_row, d_row):
    return pltpu.make_async_copy(src.at[pl.ds(s_row, 1)], dst.at[pl.ds(d_row, 1)], sem)


def _dispatch_kernel(dest_ref, h_ref, xs_in, xs_out, sem, *, td):
    del xs_in

    def issue(t, c):
        for kk in range(TOP_K):
            _row_copy(h_ref, xs_out, sem, t, dest_ref[0, 0, TOP_K * t + kk]).start(priority=kk)
        return c

    lax.fori_loop(0, td, issue, 0, unroll=ISSUE_UNROLL)
    for kk in range(TOP_K):
        pltpu.make_async_copy(h_ref, xs_out.at[pl.ds(0, td)], sem).wait()


def _dispatch(h_packed, dest, n_rows, td):
    n_tok, width = h_packed.shape
    dest3 = dest.reshape(n_tok // td, 1, TOP_K * td)
    xs0 = jnp.zeros((n_rows, width), h_packed.dtype)
    return pl.pallas_call(
        functools.partial(_dispatch_kernel, td=td), grid=(n_tok // td,),
        in_specs=[pl.BlockSpec((1, 1, TOP_K * td), lambda i: (i, 0, 0), memory_space=pltpu.SMEM),
                  pl.BlockSpec((td, width), lambda i: (i, 0)), pl.BlockSpec(memory_space=pl.ANY)],
        out_specs=pl.BlockSpec(memory_space=pl.ANY),
        out_shape=jax.ShapeDtypeStruct((n_rows, width), h_packed.dtype),
        scratch_shapes=[pltpu.SemaphoreType.DMA(())],
        input_output_aliases={2: 0},
        compiler_params=_cparams(("arbitrary",)), name="moe_dispatch",
    )(dest3, h_packed, xs0)


def _experts_kernel(be_ref, nu_ref, xs_ref, w_gu, w_dn, ys_ref):
    i = pl.program_id(0)

    @pl.when(i < nu_ref[0])
    def _():
        packed = xs_ref[...]
        lo = lax.bitcast_convert_type(packed << 16, F32)
        hi = lax.bitcast_convert_type(packed & jnp.uint32(0xFFFF0000), F32)
        xb = jnp.concatenate([lo, hi], axis=-1).astype(BF16)
        gu = _nn(xb, w_gu[0])
        d_e = gu.shape[1] // 2
        g = gu[:, :d_e]
        a = g * (1.0 / (1.0 + jnp.exp(-g))) * gu[:, d_e:]
        ys_ref[...] = _nn(a.astype(BF16), w_dn[0])

    @pl.when(i >= nu_ref[0])
    def _():
        ys_ref[...] = jnp.zeros_like(ys_ref)


def _experts(xs, block_expert, n_used, lw, bm):
    n_rows, width = xs.shape
    d_model = 2 * width
    w_gu, w_dn = lw['w_gu'], lw['w_dn']
    grid_spec = pltpu.PrefetchScalarGridSpec(
        num_scalar_prefetch=2, grid=(n_rows // bm,),
        in_specs=[pl.BlockSpec((bm, width), lambda i, be, nu: (i, 0)),
                  pl.BlockSpec((1,) + w_gu.shape[1:], lambda i, be, nu: (be[i], 0, 0)),
                  pl.BlockSpec((1,) + w_dn.shape[1:], lambda i, be, nu: (be[i], 0, 0))],
        out_specs=pl.BlockSpec((bm, d_model), lambda i, be, nu: (i, 0)))
    return pl.pallas_call(
        _experts_kernel, grid_spec=grid_spec,
        out_shape=jax.ShapeDtypeStruct((n_rows, d_model), F32),
        compiler_params=_cparams(("arbitrary",)), name="moe_experts",
    )(block_expert, n_used, xs, w_gu, w_dn)


def _combine_kernel(dest_ref, ys_ref, x_ref, route_ref, o_ref, buf, sem, *, tc):
    def issue(t, c):
        for kk in range(TOP_K):
            pltpu.make_async_copy(ys_ref.at[pl.ds(dest_ref[0, 0, TOP_K * t + kk], 1)],
                                  buf.at[kk, pl.ds(t, 1)], sem).start(priority=kk)
        return c

    lax.fori_loop(0, tc, issue, 0, unroll=ISSUE_UNROLL)
    for kk in range(TOP_K):
        pltpu.make_async_copy(ys_ref.at[pl.ds(0, tc)], buf.at[kk], sem).wait()
    route = route_ref[...]
    o_ref[...] = x_ref[...] + route[:, 2:3] * buf[0] + route[:, 3:4] * buf[1]


def _combine(ys, dest, x, route, tc):
    n_tok, d_model = x.shape
    dest3 = dest.reshape(n_tok // tc, 1, TOP_K * tc)
    return pl.pallas_call(
        functools.partial(_combine_kernel, tc=tc), grid=(n_tok // tc,),
        in_specs=[pl.BlockSpec((1, 1, TOP_K * tc), lambda i: (i, 0, 0), memory_space=pltpu.SMEM),
                  pl.BlockSpec(memory_space=pl.ANY),
                  pl.BlockSpec((tc, d_model), lambda i: (i, 0)),
                  pl.BlockSpec((tc, LANES), lambda i: (i, 0))],
        out_specs=pl.BlockSpec((tc, d_model), lambda i: (i, 0)),
        out_shape=jax.ShapeDtypeStruct((n_tok, d_model), F32),
        scratch_shapes=[pltpu.VMEM((TOP_K, tc, d_model), F32), pltpu.SemaphoreType.DMA(())],
        compiler_params=_cparams(("arbitrary",)), name="moe_combine",
    )(dest3, ys, x, route)


def _moe(x2, h_packed, route, counts, lw, cfg):
    bsz, seq, d_model = x2.shape
    n_tok = bsz * seq
    bm = cfg['bm']
    cnt = counts[0, :N_EXPERTS].astype(jnp.int32)
    padded = ((cnt + bm - 1) // bm) * bm
    ends = jnp.cumsum(padded)
    starts = ends - padded
    n_blk = -(-(n_tok * TOP_K + N_EXPERTS * (bm - 1)) // bm)
    route2 = route.reshape(n_tok, LANES)
    expert = route2[:, 0:TOP_K].astype(jnp.int32)
    rank = route2[:, 4:4 + TOP_K].astype(jnp.int32)
    e_ids = jnp.arange(N_EXPERTS, dtype=jnp.int32)
    start_of = jnp.sum(jnp.where(expert[..., None] == e_ids, starts, 0), axis=-1)
    dest = (start_of + rank).reshape(-1)
    blk_row = jnp.arange(n_blk, dtype=jnp.int32) * bm
    block_expert = jnp.minimum(jnp.sum((ends[None, :] <= blk_row[:, None]).astype(jnp.int32), axis=-1), N_EXPERTS - 1)
    n_used = (ends[-1:] // bm).astype(jnp.int32)
    xs = _dispatch(h_packed.reshape(n_tok, d_model // 2), dest, n_blk * bm, cfg['td'])
    ys = _experts(xs, block_expert, n_used, lw, bm)
    out = _combine(ys, dest, x2.reshape(n_tok, d_model), route2, cfg['tc'])
    return out.reshape(bsz, seq, d_model)


def _config(seq):
    if seq % 2048 == 0:
        return dict(ts=512, tq_a=512, tq_b=2048, tk=512, wp=256, interleave=True, bm=256, td=256, tc=256)
    return dict(ts=128, tq_a=128, tq_b=256, tk=64, wp=128, interleave=True, bm=64, td=64, tc=64)


def _trunk(x, mem, layers, cfg):
    rt = _rope_tables(x.shape[1])
    for lw in layers:
        qa, ka, va, qb, kb, vb = _in_proj(x, lw, rt, cfg['ts'])
        att = dict(tk=cfg['tk'], wp=cfg['wp'], interleave=cfg['interleave'])
        oa = _attention(qa, ka, va, dq=A_HEAD_DIM, tqb=cfg['tq_a'], name="attn_a", **att)
        ob = _attention(qb, kb, vb, dq=LANES, tqb=cfg['tq_b'], name="attn_b", **att)
        mk, mv = _mem_kv(mem, lw)
        x2, h_packed, route, counts = _post_attn(x, oa, ob, mk, mv, lw, cfg['ts'])
        x = _moe(x2, h_packed, route, counts, lw, cfg)
    return x


def kernel(x_prompt, x_sample, mem_prompt, mem_sample, norm_mix, w_in, a_q_norm, a_k_norm, b_cq_norm, b_ckv_norm, b_w_uq, b_w_ukv, b_q_norm, b_k_norm, w_out, norm_mem_x, norm_mem_m, m_w_q, m_w_kv, m_q_norm, m_k_norm, m_w_o, norm_ffn, r_coarse_w, r_coarse_b, r_fine_w, r_fine_b, e_w_gate, e_w_up, e_w_down):
    p = dict(norm_mix=norm_mix, w_in=w_in, a_q_norm=a_q_norm, a_k_norm=a_k_norm, b_cq_norm=b_cq_norm,
             b_ckv_norm=b_ckv_norm, b_w_uq=b_w_uq, b_w_ukv=b_w_ukv, b_q_norm=b_q_norm, b_k_norm=b_k_norm,
             w_out=w_out, norm_mem_x=norm_mem_x, norm_mem_m=norm_mem_m, m_w_q=m_w_q, m_w_kv=m_w_kv,
             m_q_norm=m_q_norm, m_k_norm=m_k_norm, m_w_o=m_w_o, norm_ffn=norm_ffn, r_coarse_w=r_coarse_w,
             r_coarse_b=r_coarse_b, r_fine_w=r_fine_w, r_fine_b=r_fine_b, e_w_gate=e_w_gate, e_w_up=e_w_up,
             e_w_down=e_w_down)
    layers = [_prep_layer(p, i) for i in range(w_in.shape[0])]
    y_prompt = _trunk(x_prompt, mem_prompt, layers, _config(x_prompt.shape[1]))
    y_sample = _trunk(x_sample, mem_sample, layers, _config(x_sample.shape[1]))
    return (y_prompt, y_sample)
```

```python
import functools

import numpy as np
import jax
import jax.numpy as jnp
from jax import lax
from jax.experimental import pallas as pl
from jax.experimental.pallas import tpu as pltpu

GRID_W = 64
ROPE_THETA = 10000.0
EPS = 1e-6
LOG2E = 1.4426950408889634

A_HEADS, A_KV_HEADS, A_HEAD_DIM = 8, 2, 64
B_HEADS, B_NOPE, B_ROPE, B_V = 8, 64, 32, 64
B_QK = B_NOPE + B_ROPE
B_Q_LORA, B_KV_LORA = 256, 128
M_HEADS, M_HEAD_DIM = 4, 128
N_GROUPS, EXPERTS_PER_GROUP, TOP_K = 4, 8, 2
N_EXPERTS = N_GROUPS * EXPERTS_PER_GROUP
LANES = 128
SUM_ROWS = 16
ISSUE_UNROLL = 8

VMEM_LIMIT = 56 * 1024 * 1024

BF16 = jnp.bfloat16
F32 = jnp.float32


def _cparams(sem, flags=None):
    return pltpu.CompilerParams(dimension_semantics=sem, vmem_limit_bytes=VMEM_LIMIT, flags=flags)


def _nt(a, b):
    return lax.dot_general(a, b, (((1,), (1,)), ((), ())), preferred_element_type=F32)


def _tn(a, b):
    return lax.dot_general(a, b, (((0,), (0,)), ((), ())), preferred_element_type=F32)


def _nn(a, b):
    return jnp.dot(a, b, preferred_element_type=F32)


def _rope_angles(seq_len, rot_dim):
    rows = seq_len // GRID_W
    r = jnp.repeat(jnp.arange(rows, dtype=F32), GRID_W)
    c = jnp.tile(jnp.arange(GRID_W, dtype=F32), rows)
    axis_dim = rot_dim // 2
    inv = ROPE_THETA ** (-(jnp.arange(axis_dim // 2, dtype=F32) * 2.0) / axis_dim)
    return jnp.concatenate([r[:, None] * inv, c[:, None] * inv], axis=-1)


def _rope_tables(seq_len):
    ang_a = _rope_angles(seq_len, A_HEAD_DIM)
    ang_b = _rope_angles(seq_len, B_ROPE)
    ca, sa = jnp.cos(ang_a), jnp.sin(ang_a)
    cb, sb = jnp.cos(ang_b), jnp.sin(ang_b)
    z64 = jnp.zeros((seq_len, 64), F32)
    z32 = jnp.zeros((seq_len, 32), F32)
    one64 = jnp.ones((seq_len, 64), F32)
    return dict(
        cos_a_t=ca.T, sin_a_t=sa.T,
        cos_b_t=cb.T, sin_b_t=sb.T,
        cos_ka=jnp.concatenate([ca, ca, z64], -1),
        sin_ka=jnp.concatenate([-sa, sa, z64], -1),
        cos_kb=jnp.concatenate([one64, cb, cb, z32], -1),
        sin_kb=jnp.concatenate([z64, -sb, sb, z32], -1),
    )


_EVEN64, _ODD64 = np.arange(0, 64, 2), np.arange(1, 64, 2)
_PERM64 = np.concatenate([_EVEN64, _ODD64])
_SWAP64 = np.concatenate([_ODD64, _EVEN64])
_EVEN32, _ODD32 = np.arange(0, 32, 2), np.arange(1, 32, 2)
_PERM32 = np.concatenate([_EVEN32, _ODD32])
_SWAP32 = np.concatenate([_ODD32, _EVEN32])


def _prep_layer(p, i):
    d_model = p['w_in'].shape[1]
    w_in = p['w_in'][i]
    o_ak = A_HEADS * A_HEAD_DIM
    o_av = o_ak + A_KV_HEADS * A_HEAD_DIM
    o_cq = o_av + A_KV_HEADS * A_HEAD_DIM
    o_ckv = o_cq + B_Q_LORA
    o_kr = o_ckv + B_KV_LORA
    zc64 = jnp.zeros((d_model, 64), F32)
    zc32 = jnp.zeros((d_model, 32), F32)

    aq_cols = np.concatenate([h * A_HEAD_DIM + _PERM64 for h in range(A_HEADS)])
    w_d = jnp.concatenate([w_in[:, aq_cols], w_in[:, o_av:o_cq], w_in[:, o_cq:o_ckv], w_in[:, o_ckv:o_kr]], -1)
    ak, akp = [], []
    for j in range(A_KV_HEADS):
        ak += [w_in[:, o_ak + j * A_HEAD_DIM + _PERM64], zc64]
        akp += [w_in[:, o_ak + j * A_HEAD_DIM + _SWAP64], zc64]
    kr = [zc64, w_in[:, o_kr + _PERM32], zc32]
    krp = [zc64, w_in[:, o_kr + _SWAP32], zc32]
    w_t = jnp.concatenate(ak + akp + [w_in[:, o_ckv:o_kr]] + kr + krp, -1)

    z64 = jnp.zeros((64,), F32)
    z32 = jnp.zeros((32,), F32)
    a_k = p['a_k_norm'][i]
    b_q = p['b_q_norm'][i]
    b_k = p['b_k_norm'][i]
    w_uq = p['b_w_uq'][i]
    uq_cols = np.concatenate([np.concatenate([h * B_QK + np.arange(B_NOPE), h * B_QK + B_NOPE + _PERM32])
                              for h in range(B_HEADS)])
    w_ukv = p['b_w_ukv'][i]
    zk = jnp.zeros((B_KV_LORA, 64), F32)
    wk_pad = jnp.concatenate(
        sum([[w_ukv[:, h * (B_NOPE + B_V):h * (B_NOPE + B_V) + B_NOPE], zk] for h in range(B_HEADS)], []), -1)
    wv = jnp.concatenate(
        [w_ukv[:, h * (B_NOPE + B_V) + B_NOPE:(h + 1) * (B_NOPE + B_V)] for h in range(B_HEADS)], -1)

    r_w = jnp.concatenate([p['r_coarse_w'][i], jnp.zeros((d_model, LANES - N_GROUPS), F32),
                           p['r_fine_w'][i], jnp.zeros((d_model, LANES - N_EXPERTS), F32)], -1)
    r_hi = r_w.astype(BF16)
    r_lo = (r_w - r_hi.astype(F32)).astype(BF16)
    r_b = jnp.concatenate([p['r_coarse_b'][i], jnp.zeros((LANES - N_GROUPS,), F32),
                           p['r_fine_b'][i], jnp.zeros((LANES - N_EXPERTS,), F32)])[None, :]
    return dict(
        g_mix=p['norm_mix'][i][None, :],
        w_d_t=w_d.T.astype(BF16),
        w_t=w_t.astype(BF16),
        g_aq=p['a_q_norm'][i][_PERM64][:, None],
        g_ak=jnp.concatenate([a_k[_PERM64], z64])[None, :],
        g_akp=jnp.concatenate([a_k[_SWAP64], z64])[None, :],
        g_cq=p['b_cq_norm'][i][:, None],
        g_ckv_row=p['b_ckv_norm'][i][None, :],
        g_ckv_col=p['b_ckv_norm'][i][:, None],
        w_uq_t=w_uq[:, uq_cols].T.astype(BF16),
        g_bq=jnp.concatenate([b_q[:B_NOPE], b_q[B_NOPE + _PERM32]])[:, None],
        wk_pad=wk_pad.astype(BF16),
        wv_t=wv.T.astype(BF16),
        g_bk=jnp.concatenate([b_k[:B_NOPE], b_k[B_NOPE + _PERM32], z32])[None, :],
        g_bkp=jnp.concatenate([z64, b_k[B_NOPE + _SWAP32], z32])[None, :],
        w_out=p['w_out'][i].astype(BF16),
        g_memx=p['norm_mem_x'][i][None, :],
        g_memm=p['norm_mem_m'][i][None, :],
        m_w_q=p['m_w_q'][i].astype(BF16),
        m_w_kv=p['m_w_kv'][i].astype(BF16),
        g_mq=p['m_q_norm'][i][None, :],
        g_mk=p['m_k_norm'][i][None, :],
        m_w_o=p['m_w_o'][i].astype(BF16),
        g_ffn=p['norm_ffn'][i][None, :],
        r_hi=r_hi, r_lo=r_lo, r_b=r_b,
        w_gu=jnp.concatenate([p['e_w_gate'][i], p['e_w_up'][i]], -1).astype(BF16),
        w_dn=p['e_w_down'][i].astype(BF16),
    )


def _in_proj_kernel(x_ref, g_mix, w_d_t, w_t, g_aq, g_ak, g_akp, g_cq, g_ckv_row, g_ckv_col, w_uq_t, g_bq,
                    wk_pad, wv_t, g_bk, g_bkp, cos_a_t, sin_a_t, cos_b_t, sin_b_t, cos_ka, sin_ka, cos_kb, sin_kb,
                    qa_ref, ka_ref, va_ref, qb_ref, kb_ref, vb_ref):
    x = x_ref[0]
    h = (x * lax.rsqrt(jnp.mean(x * x, axis=-1, keepdims=True) + EPS) * g_mix[...]).astype(BF16)
    pd = _nt(w_d_t[...], h)
    pt = _nn(h, w_t[...])
    ts = x.shape[0]

    ca, sa = cos_a_t[...], sin_a_t[...]
    qscale = A_HEAD_DIM ** -0.5 * LOG2E
    for hd in range(A_HEADS):
        q = pd[hd * 64:(hd + 1) * 64]
        qn = q * lax.rsqrt(jnp.mean(q * q, axis=0, keepdims=True) + EPS) * g_aq[...]
        x1, x2 = qn[:32], qn[32:]
        rot = jnp.concatenate([x1 * ca - x2 * sa, x1 * sa + x2 * ca], axis=0) * qscale
        qa_ref[0, hd] = rot.astype(BF16)
    o_av = A_HEADS * A_HEAD_DIM
    for j in range(A_KV_HEADS):
        va_ref[0, j] = pd[o_av + j * 64:o_av + (j + 1) * 64].astype(BF16)
    for j in range(A_KV_HEADS):
        k = pt[:, j * LANES:(j + 1) * LANES]
        kp = pt[:, (A_KV_HEADS + j) * LANES:(A_KV_HEADS + j + 1) * LANES]
        rs = lax.rsqrt(jnp.sum(k * k, axis=-1, keepdims=True) * (1.0 / A_HEAD_DIM) + EPS)
        ka_ref[0, j] = (rs * (k * g_ak[...] * cos_ka[...] + kp * g_akp[...] * sin_ka[...])).astype(BF16)

    o_cq = o_av + A_KV_HEADS * A_HEAD_DIM
    cq = pd[o_cq:o_cq + B_Q_LORA]
    cqn = (cq * lax.rsqrt(jnp.mean(cq * cq, axis=0, keepdims=True) + EPS) * g_cq[...]).astype(BF16)
    bq = _nn(w_uq_t[...], cqn)
    cb, sb = cos_b_t[...], sin_b_t[...]
    bscale = B_QK ** -0.5 * LOG2E
    zpad = jnp.zeros((LANES - B_QK, ts), F32)
    for hd in range(B_HEADS):
        q = bq[hd * B_QK:(hd + 1) * B_QK]
        qn = q * lax.rsqrt(jnp.mean(q * q, axis=0, keepdims=True) + EPS) * g_bq[...]
        x1, x2 = qn[B_NOPE:B_NOPE + 16], qn[B_NOPE + 16:]
        rot = jnp.concatenate([qn[:B_NOPE], x1 * cb - x2 * sb, x1 * sb + x2 * cb], axis=0) * bscale
        qb_ref[0, hd] = jnp.concatenate([rot, zpad], axis=0).astype(BF16)

    o_ckv_t = 2 * A_KV_HEADS * LANES
    ckv = pt[:, o_ckv_t:o_ckv_t + B_KV_LORA]
    ckvn = (ckv * lax.rsqrt(jnp.mean(ckv * ckv, axis=-1, keepdims=True) + EPS) * g_ckv_row[...]).astype(BF16)
    k_nope = _nn(ckvn, wk_pad[...])
    kr = pt[:, o_ckv_t + LANES:o_ckv_t + 2 * LANES]
    krp = pt[:, o_ckv_t + 2 * LANES:o_ckv_t + 3 * LANES]
    rope_part = krp * g_bkp[...] * sin_kb[...]
    for hd in range(B_HEADS):
        k = k_nope[:, hd * LANES:(hd + 1) * LANES] + kr
        rs = lax.rsqrt(jnp.sum(k * k, axis=-1, keepdims=True) * (1.0 / B_QK) + EPS)
        kb_ref[0, hd] = (rs * (k * g_bk[...] * cos_kb[...] + rope_part)).astype(BF16)
    o_ckv = o_cq + B_Q_LORA
    ckv_t = pd[o_ckv:o_ckv + B_KV_LORA]
    ckvn_t = (ckv_t * lax.rsqrt(jnp.mean(ckv_t * ckv_t, axis=0, keepdims=True) + EPS) * g_ckv_col[...]).astype(BF16)
    bv = _nn(wv_t[...], ckvn_t)
    for hd in range(B_HEADS):
        vb_ref[0, hd] = bv[hd * B_V:(hd + 1) * B_V].astype(BF16)


def _in_proj(x, lw, rt, ts):
    bsz, seq, d_model = x.shape
    grid = (bsz, seq // ts)
    full = lambda a: pl.BlockSpec(a.shape, lambda b, j: (0,) * a.ndim)
    weights = [lw[k] for k in ('g_mix', 'w_d_t', 'w_t', 'g_aq', 'g_ak', 'g_akp', 'g_cq', 'g_ckv_row', 'g_ckv_col',
                               'w_uq_t', 'g_bq', 'wk_pad', 'wv_t', 'g_bk', 'g_bkp')]
    tabs_t = [rt[k] for k in ('cos_a_t', 'sin_a_t', 'cos_b_t', 'sin_b_t')]
    tabs_k = [rt[k] for k in ('cos_ka', 'sin_ka', 'cos_kb', 'sin_kb')]
    in_specs = ([pl.BlockSpec((1, ts, d_model), lambda b, j: (b, j, 0))] + [full(w) for w in weights]
                + [pl.BlockSpec((t.shape[0], ts), lambda b, j: (0, j)) for t in tabs_t]
                + [pl.BlockSpec((ts, LANES), lambda b, j: (j, 0)) for _ in tabs_k])
    t_major = lambda nh: pl.BlockSpec((1, nh, ts, LANES), lambda b, j: (b, 0, j, 0))
    d_major = lambda nh, d: pl.BlockSpec((1, nh, d, ts), lambda b, j: (b, 0, 0, j))
    out_shape = [
        jax.ShapeDtypeStruct((bsz, A_HEADS, A_HEAD_DIM, seq), BF16),
        jax.ShapeDtypeStruct((bsz, A_KV_HEADS, seq, LANES), BF16),
        jax.ShapeDtypeStruct((bsz, A_KV_HEADS, A_HEAD_DIM, seq), BF16),
        jax.ShapeDtypeStruct((bsz, B_HEADS, LANES, seq), BF16),
        jax.ShapeDtypeStruct((bsz, B_HEADS, seq, LANES), BF16),
        jax.ShapeDtypeStruct((bsz, B_HEADS, B_V, seq), BF16),
    ]
    out_specs = [d_major(A_HEADS, A_HEAD_DIM), t_major(A_KV_HEADS), d_major(A_KV_HEADS, A_HEAD_DIM),
                 d_major(B_HEADS, LANES), t_major(B_HEADS), d_major(B_HEADS, B_V)]
    return pl.pallas_call(
        _in_proj_kernel, grid=grid, in_specs=in_specs, out_specs=out_specs, out_shape=out_shape,
        compiler_params=_cparams(("parallel", "parallel")), name="in_proj",
    )(x, *weights, *tabs_t, *tabs_k)


def _attn_kernel(q_ref, k_ref, v_ref, o_ref, s_scr, cmax_scr, *, dq, tk, wp, interleave, unroll):
    n_heads, _, tqb = q_ref.shape[1:]
    seq = k_ref.shape[2]
    dv = v_ref.shape[2]
    panels = [(g, pnl * wp) for g in range(n_heads) for pnl in range(tqb // wp)]
    n_chunks = seq // tk

    ones_rows = jnp.ones((SUM_ROWS, tk), BF16)

    def k_chunk(c):
        return k_ref[0, 0, pl.ds(pl.multiple_of(c * tk, tk), tk), :][:, :dq]

    def v_chunk(c):
        return jnp.concatenate([v_ref[0, 0, :, pl.ds(pl.multiple_of(c * tk, tk), tk)], ones_rows], axis=0)

    def score_panel(k, slot, n):
        g, lo = panels[n]
        s = _nn(k, q_ref[0, g, :, lo:lo + wp])
        s_scr[slot, n] = s
        cmax_scr[slot, n] = jnp.max(s, axis=0, keepdims=True)

    def update_panel(v, slot, n, m, acc):
        m_new = jnp.maximum(m, cmax_scr[slot, n])
        p = jnp.exp2((s_scr[slot, n] - m_new).astype(BF16))
        return m_new, jnp.exp2(m - m_new) * acc + _nn(v, p)

    def step(c_next, c_cur, slot_cur, state):
        k = None if c_next is None else k_chunk(c_next)
        v = v_chunk(c_cur)
        if k is not None and not interleave:
            for n in range(len(panels)):
                score_panel(k, 1 - slot_cur, n)
        out = []
        for n, (m, acc) in enumerate(state):
            if k is not None and interleave:
                score_panel(k, 1 - slot_cur, n)
            out.append(update_panel(v, slot_cur, n, m, acc))
        return tuple(out)

    def body(j, state):
        c = 2 * j
        state = step(c + 1, c, 0, state)
        return step(c + 2, c + 1, 1, state)

    state = tuple((jnp.full((1, wp), -jnp.inf, F32), jnp.zeros((dv + SUM_ROWS, wp), F32)) for _ in panels)
    k0 = k_chunk(0)
    for n in range(len(panels)):
        score_panel(k0, 0, n)
    state = lax.fori_loop(0, n_chunks // 2 - 1, body, state, unroll=unroll)
    state = step(n_chunks - 1, n_chunks - 2, 0, state)
    state = step(None, n_chunks - 1, 1, state)
    for (g, lo), (_, acc) in zip(panels, state):
        o_ref[0, g, :, lo:lo + wp] = (acc[:dv] * (1.0 / acc[dv:dv + 1])).astype(BF16)


def _attention(q_t, k, v_t, *, dq, tqb, tk, wp, interleave, unroll, name):
    bsz, hq, dq_rows, seq = q_t.shape
    hk = k.shape[1]
    grp = hq // hk
    dv = v_t.shape[2]
    grid = (bsz, hk, seq // tqb)
    return pl.pallas_call(
        functools.partial(_attn_kernel, dq=dq, tk=tk, wp=wp, interleave=interleave, unroll=unroll),
        grid=grid,
        in_specs=[pl.BlockSpec((1, grp, dq_rows, tqb), lambda b, j, i: (b, j, 0, i)),
                  pl.BlockSpec((1, 1, seq, LANES), lambda b, j, i: (b, j, 0, 0)),
                  pl.BlockSpec((1, 1, dv, seq), lambda b, j, i: (b, j, 0, 0))],
        out_specs=pl.BlockSpec((1, grp, dv, tqb), lambda b, j, i: (b, j, 0, i)),
        out_shape=jax.ShapeDtypeStruct((bsz, hq, dv, seq), BF16),
        scratch_shapes=[pltpu.VMEM((2, grp * (tqb // wp), tk, wp), F32),
                        pltpu.VMEM((2, grp * (tqb // wp), 1, wp), F32)],
        compiler_params=_cparams(("parallel", "parallel", "parallel")), name=name,
    )(q_t, k, v_t)


def _mem_kv_kernel(mem_ref, g_memm, w_kv, g_mk, k_ref, v_ref):
    m = mem_ref[0]
    mn = (m * lax.rsqrt(jnp.mean(m * m, axis=-1, keepdims=True) + EPS) * g_memm[...]).astype(BF16)
    kv = _nn(mn, w_kv[...])
    hw = M_HEADS * M_HEAD_DIM
    for hh in range(M_HEADS):
        k = kv[:, hh * M_HEAD_DIM:(hh + 1) * M_HEAD_DIM]
        kn = k * lax.rsqrt(jnp.mean(k * k, axis=-1, keepdims=True) + EPS) * g_mk[...]
        k_ref[0, :, hh * M_HEAD_DIM:(hh + 1) * M_HEAD_DIM] = kn.astype(BF16)
    v_ref[0] = kv[:, hw:].astype(BF16)


def _mem_kv(mem, lw):
    bsz, n_mem, d_model = mem.shape
    hw = M_HEADS * M_HEAD_DIM
    full = lambda a: pl.BlockSpec(a.shape, lambda b: (0,) * a.ndim)
    ws = [lw['g_memm'], lw['m_w_kv'], lw['g_mk']]
    return pl.pallas_call(
        _mem_kv_kernel, grid=(bsz,),
        in_specs=[pl.BlockSpec((1, n_mem, d_model), lambda b: (b, 0, 0))] + [full(w) for w in ws],
        out_specs=[pl.BlockSpec((1, n_mem, hw), lambda b: (b, 0, 0))] * 2,
        out_shape=[jax.ShapeDtypeStruct((bsz, n_mem, hw), BF16)] * 2,
        compiler_params=_cparams(("parallel",)), name="mem_kv",
    )(mem, *ws)


def _pack_bf16_pairs(lo, hi):
    lo_b = lax.bitcast_convert_type(lo.astype(BF16).astype(F32), jnp.uint32)
    hi_b = lax.bitcast_convert_type(hi.astype(BF16).astype(F32), jnp.uint32)
    return (hi_b & jnp.uint32(0xFFFF0000)) | (lo_b >> 16)


def _post_attn_kernel(x_ref, oa_ref, ob_ref, mk_ref, mv_ref, w_out, g_memx, m_w_q, g_mq, m_w_o, g_ffn,
                      r_hi, r_lo, r_b, x_out, h_out, route_out, cnt_out, cnt_scr):
    first = jnp.logical_and(pl.program_id(0) == 0, pl.program_id(1) == 0)

    @pl.when(first)
    def _():
        cnt_scr[...] = jnp.zeros_like(cnt_scr)

    ts = x_ref.shape[1]
    half = A_HEADS * A_HEAD_DIM
    oa = oa_ref[0].reshape(half, ts)
    ob = ob_ref[0].reshape(half, ts)
    x1 = x_ref[0] + _tn(oa, w_out[:half, :]) + _tn(ob, w_out[half:, :])

    h2 = (x1 * lax.rsqrt(jnp.mean(x1 * x1, axis=-1, keepdims=True) + EPS) * g_memx[...]).astype(BF16)
    qm = _nn(h2, m_w_q[...])
    mscale = M_HEAD_DIM ** -0.5 * LOG2E
    heads = []
    for hh in range(M_HEADS):
        sl = slice(hh * M_HEAD_DIM, (hh + 1) * M_HEAD_DIM)
        q = qm[:, sl]
        qn = (q * lax.rsqrt(jnp.mean(q * q, axis=-1, keepdims=True) + EPS) * g_mq[...] * mscale).astype(BF16)
        s = _nt(qn, mk_ref[0, :, sl])
        p = jnp.exp2(s - jnp.max(s, axis=-1, keepdims=True))
        o = _nn(p.astype(BF16), mv_ref[0, :, sl]) * (1.0 / jnp.sum(p, axis=-1, keepdims=True))
        heads.append(o.astype(BF16))
    x2 = x1 + _nn(jnp.concatenate(heads, axis=-1), m_w_o[...])
    x_out[0] = x2

    h3 = x2 * lax.rsqrt(jnp.mean(x2 * x2, axis=-1, keepdims=True) + EPS) * g_ffn[...]
    d_half = h3.shape[1] // 2
    h_out[0] = _pack_bf16_pairs(h3[:, :d_half], h3[:, d_half:])

    h_hi = h3.astype(BF16)
    h_lo = (h3 - h_hi.astype(F32)).astype(BF16)
    logits = _nn(h_hi, r_hi[...]) + _nn(h_hi, r_lo[...]) + _nn(h_lo, r_hi[...]) + r_b[...]
    lane = lax.broadcasted_iota(jnp.int32, (ts, LANES), 1)
    neg = jnp.float32(-jnp.inf)
    lc = jnp.where(lane < N_GROUPS, logits[:, :LANES], neg)
    mc = jnp.max(lc, axis=-1, keepdims=True)
    g_idx = jnp.min(jnp.where(lc == mc, lane, LANES), axis=-1, keepdims=True)
    g_w = 1.0 / jnp.sum(jnp.exp(lc - mc), axis=-1, keepdims=True)
    lf = jnp.where((lane >> 3) == g_idx, logits[:, LANES:], neg)
    m1 = jnp.max(lf, axis=-1, keepdims=True)
    i1 = jnp.min(jnp.where(lf == m1, lane, LANES), axis=-1, keepdims=True)
    lf2 = jnp.where(lane == i1, neg, lf)
    m2 = jnp.max(lf2, axis=-1, keepdims=True)
    i2 = jnp.min(jnp.where(lf2 == m2, lane, LANES), axis=-1, keepdims=True)
    r21 = jnp.exp(m2 - m1)
    gate1 = g_w / (1.0 + r21)
    gate2 = g_w * r21 / (1.0 + r21)

    hot1 = lane == i1
    hot2 = lane == i2
    onehot = jnp.where(jnp.logical_or(hot1, hot2), 1.0, 0.0)
    row = lax.broadcasted_iota(jnp.int32, (ts, ts), 0)
    col = lax.broadcasted_iota(jnp.int32, (ts, ts), 1)
    tri = jnp.where(col < row, 1.0, 0.0).astype(BF16)
    base = _nn(tri, onehot.astype(BF16)) + cnt_scr[...]
    rank1 = jnp.sum(jnp.where(hot1, base, 0.0), axis=-1, keepdims=True)
    rank2 = jnp.sum(jnp.where(hot2, base, 0.0), axis=-1, keepdims=True)
    cnt_scr[...] += jnp.sum(onehot, axis=0, keepdims=True)
    cnt_out[...] = cnt_scr[...]

    vals = (i1.astype(F32), i2.astype(F32), gate1, gate2, rank1, rank2)
    route = jnp.zeros((ts, LANES), F32)
    for n, v in enumerate(vals):
        route = jnp.where(lane == n, v, route)
    route_out[0] = route


def _post_attn(x, oa, ob, mk, mv, lw, ts):
    bsz, seq, d_model = x.shape
    n_mem, hw = mk.shape[1:]
    grid = (bsz, seq // ts)
    full = lambda a: pl.BlockSpec(a.shape, lambda b, j: (0,) * a.ndim)
    ws = [lw[k] for k in ('w_out', 'g_memx', 'm_w_q', 'g_mq', 'm_w_o', 'g_ffn', 'r_hi', 'r_lo', 'r_b')]
    tok = lambda w: pl.BlockSpec((1, ts, w), lambda b, j: (b, j, 0))
    return pl.pallas_call(
        _post_attn_kernel, grid=grid,
        in_specs=[tok(d_model),
                  pl.BlockSpec((1, A_HEADS, A_HEAD_DIM, ts), lambda b, j: (b, 0, 0, j)),
                  pl.BlockSpec((1, B_HEADS, B_V, ts), lambda b, j: (b, 0, 0, j)),
                  pl.BlockSpec((1, n_mem, hw), lambda b, j: (b, 0, 0)),
                  pl.BlockSpec((1, n_mem, hw), lambda b, j: (b, 0, 0))] + [full(w) for w in ws],
        out_specs=[tok(d_model), tok(d_model // 2), tok(LANES), pl.BlockSpec((1, LANES), lambda b, j: (0, 0))],
        out_shape=[jax.ShapeDtypeStruct((bsz, seq, d_model), F32),
                   jax.ShapeDtypeStruct((bsz, seq, d_model // 2), jnp.uint32),
                   jax.ShapeDtypeStruct((bsz, seq, LANES), F32),
                   jax.ShapeDtypeStruct((1, LANES), F32)],
        scratch_shapes=[pltpu.VMEM((1, LANES), F32)],
        compiler_params=_cparams(("arbitrary", "arbitrary")), name="post_attn",
    )(x, oa, ob, mk, mv, *ws)


def _row_copy(src, dst, sem, s_row, d_row):
    return pltpu.make_async_copy(src.at[pl.ds(s_row, 1)], dst.at[pl.ds(d_row, 1)], sem)


def _dispatch_kernel(dest_ref, h_ref, xs_in, xs_out, sem, *, td):
    del xs_in

    def issue(t, c):
        for kk in range(TOP_K):
            _row_copy(h_ref, xs_out, sem, t, dest_ref[0, 0, TOP_K * t + kk]).start(priority=kk)
        return c

    lax.fori_loop(0, td, issue, 0, unroll=ISSUE_UNROLL)
    for kk in range(TOP_K):
        pltpu.make_async_copy(h_ref, xs_out.at[pl.ds(0, td)], sem).wait()


def _dispatch(h_packed, dest, n_rows, td):
    n_tok, width = h_packed.shape
    dest3 = dest.reshape(n_tok // td, 1, TOP_K * td)
    xs0 = jnp.zeros((n_rows, width), h_packed.dtype)
    return pl.pallas_call(
        functools.partial(_dispatch_kernel, td=td), grid=(n_tok // td,),
        in_specs=[pl.BlockSpec((1, 1, TOP_K * td), lambda i: (i, 0, 0), memory_space=pltpu.SMEM),
                  pl.BlockSpec((td, width), lambda i: (i, 0)), pl.BlockSpec(memory_space=pl.ANY)],
        out_specs=pl.BlockSpec(memory_space=pl.ANY),
        out_shape=jax.ShapeDtypeStruct((n_rows, width), h_packed.dtype),
        scratch_shapes=[pltpu.SemaphoreType.DMA(())],
        input_output_aliases={2: 0},
        compiler_params=_cparams(("arbitrary",)), name="moe_dispatch",
    )(dest3, h_packed, xs0)


def _experts_kernel(be_ref, nu_ref, xs_ref, w_gu, w_dn, ys_ref):
    i = pl.program_id(0)

    @pl.when(i < nu_ref[0])
    def _():
        packed = xs_ref[...]
        lo = lax.bitcast_convert_type(packed << 16, F32)
        hi = lax.bitcast_convert_type(packed & jnp.uint32(0xFFFF0000), F32)
        xb = jnp.concatenate([lo, hi], axis=-1).astype(BF16)
        gu = _nn(xb, w_gu[0])
        d_e = gu.shape[1] // 2
        g = gu[:, :d_e]
        a = g * (1.0 / (1.0 + jnp.exp(-g))) * gu[:, d_e:]
        ys_ref[...] = _nn(a.astype(BF16), w_dn[0])

    @pl.when(i >= nu_ref[0])
    def _():
        ys_ref[...] = jnp.zeros_like(ys_ref)


def _experts(xs, block_expert, n_used, lw, bm):
    n_rows, width = xs.shape
    d_model = 2 * width
    w_gu, w_dn = lw['w_gu'], lw['w_dn']
    grid_spec = pltpu.PrefetchScalarGridSpec(
        num_scalar_prefetch=2, grid=(n_rows // bm,),
        in_specs=[pl.BlockSpec((bm, width), lambda i, be, nu: (i, 0)),
                  pl.BlockSpec((1,) + w_gu.shape[1:], lambda i, be, nu: (be[i], 0, 0)),
                  pl.BlockSpec((1,) + w_dn.shape[1:], lambda i, be, nu: (be[i], 0, 0))],
        out_specs=pl.BlockSpec((bm, d_model), lambda i, be, nu: (i, 0)))
    return pl.pallas_call(
        _experts_kernel, grid_spec=grid_spec,
        out_shape=jax.ShapeDtypeStruct((n_rows, d_model), F32),
        compiler_params=_cparams(("arbitrary",)), name="moe_experts",
    )(block_expert, n_used, xs, w_gu, w_dn)


def _combine_kernel(dest_ref, ys_ref, x_ref, route_ref, o_ref, buf, sem, *, tc):
    def issue(t, c):
        for kk in range(TOP_K):
            pltpu.make_async_copy(ys_ref.at[pl.ds(dest_ref[0, 0, TOP_K * t + kk], 1)],
                                  buf.at[kk, pl.ds(t, 1)], sem).start(priority=kk)
        return c

    lax.fori_loop(0, tc, issue, 0, unroll=ISSUE_UNROLL)
    for kk in range(TOP_K):
        pltpu.make_async_copy(ys_ref.at[pl.ds(0, tc)], buf.at[kk], sem).wait()
    route = route_ref[...]
    o_ref[...] = x_ref[...] + route[:, 2:3] * buf[0] + route[:, 3:4] * buf[1]


def _combine(ys, dest, x, route, tc):
    n_tok, d_model = x.shape
    dest3 = dest.reshape(n_tok // tc, 1, TOP_K * tc)
    return pl.pallas_call(
        functools.partial(_combine_kernel, tc=tc), grid=(n_tok // tc,),
        in_specs=[pl.BlockSpec((1, 1, TOP_K * tc), lambda i: (i, 0, 0), memory_space=pltpu.SMEM),
                  pl.BlockSpec(memory_space=pl.ANY),
                  pl.BlockSpec((tc, d_model), lambda i: (i, 0)),
                  pl.BlockSpec((tc, LANES), lambda i: (i, 0))],
        out_specs=pl.BlockSpec((tc, d_model), lambda i: (i, 0)),
        out_shape=jax.ShapeDtypeStruct((n_tok, d_model), F32),
        scratch_shapes=[pltpu.VMEM((TOP_K, tc, d_model), F32), pltpu.SemaphoreType.DMA(())],
        compiler_params=_cparams(("arbitrary",)), name="moe_combine",
    )(dest3, ys, x, route)


def _moe(x2, h_packed, route, counts, lw, cfg):
    bsz, seq, d_model = x2.shape
    n_tok = bsz * seq
    bm = cfg['bm']
    cnt = counts[0, :N_EXPERTS].astype(jnp.int32)
    padded = ((cnt + bm - 1) // bm) * bm
    ends = jnp.cumsum(padded)
    starts = ends - padded
    n_blk = -(-(n_tok * TOP_K + N_EXPERTS * (bm - 1)) // bm)
    route2 = route.reshape(n_tok, LANES)
    expert = route2[:, 0:TOP_K].astype(jnp.int32)
    rank = route2[:, 4:4 + TOP_K].astype(jnp.int32)
    e_ids = jnp.arange(N_EXPERTS, dtype=jnp.int32)
    start_of = jnp.sum(jnp.where(expert[..., None] == e_ids, starts, 0), axis=-1)
    dest = (start_of + rank).reshape(-1)
    blk_row = jnp.arange(n_blk, dtype=jnp.int32) * bm
    block_expert = jnp.minimum(jnp.sum((ends[None, :] <= blk_row[:, None]).astype(jnp.int32), axis=-1), N_EXPERTS - 1)
    n_used = (ends[-1:] // bm).astype(jnp.int32)
    xs = _dispatch(h_packed.reshape(n_tok, d_model // 2), dest, n_blk * bm, cfg['td'])
    ys = _experts(xs, block_expert, n_used, lw, bm)
    out = _combine(ys, dest, x2.reshape(n_tok, d_model), route2, cfg['tc'])
    return out.reshape(bsz, seq, d_model)


def _config(seq):
    if seq % 2048 == 0:
        unroll = 4 if seq >= 8192 else 1
        return dict(ts=512, tq_a=512, tq_b=2048, tk=512, wp=256, interleave=True, unroll=unroll,
                    bm=512, td=1024, tc=1024)
    return dict(ts=128, tq_a=128, tq_b=256, tk=64, wp=128, interleave=True, unroll=1,
                bm=64, td=64, tc=64)


def _trunk(x, mem, layers, cfg):
    rt = _rope_tables(x.shape[1])
    for lw in layers:
        qa, ka, va, qb, kb, vb = _in_proj(x, lw, rt, cfg['ts'])
        att = dict(tk=cfg['tk'], wp=cfg['wp'], interleave=cfg['interleave'], unroll=cfg['unroll'])
        oa = _attention(qa, ka, va, dq=A_HEAD_DIM, tqb=cfg['tq_a'], name="attn_a", **att)
        ob = _attention(qb, kb, vb, dq=LANES, tqb=cfg['tq_b'], name="attn_b", **att)
        mk, mv = _mem_kv(mem, lw)
        x2, h_packed, route, counts = _post_attn(x, oa, ob, mk, mv, lw, cfg['ts'])
        x = _moe(x2, h_packed, route, counts, lw, cfg)
    return x


def kernel(x_prompt, x_sample, mem_prompt, mem_sample, norm_mix, w_in, a_q_norm, a_k_norm, b_cq_norm, b_ckv_norm, b_w_uq, b_w_ukv, b_q_norm, b_k_norm, w_out, norm_mem_x, norm_mem_m, m_w_q, m_w_kv, m_q_norm, m_k_norm, m_w_o, norm_ffn, r_coarse_w, r_coarse_b, r_fine_w, r_fine_b, e_w_gate, e_w_up, e_w_down):
    p = dict(norm_mix=norm_mix, w_in=w_in, a_q_norm=a_q_norm, a_k_norm=a_k_norm, b_cq_norm=b_cq_norm,
             b_ckv_norm=b_ckv_norm, b_w_uq=b_w_uq, b_w_ukv=b_w_ukv, b_q_norm=b_q_norm, b_k_norm=b_k_norm,
             w_out=w_out, norm_mem_x=norm_mem_x, norm_mem_m=norm_mem_m, m_w_q=m_w_q, m_w_kv=m_w_kv,
             m_q_norm=m_q_norm, m_k_norm=m_k_norm, m_w_o=m_w_o, norm_ffn=norm_ffn, r_coarse_w=r_coarse_w,
             r_coarse_b=r_coarse_b, r_fine_w=r_fine_w, r_fine_b=r_fine_b, e_w_gate=e_w_gate, e_w_up=e_w_up,
             e_w_down=e_w_down)
    layers = [_prep_layer(p, i) for i in range(w_in.shape[0])]
    y_prompt = _trunk(x_prompt, mem_prompt, layers, _config(x_prompt.shape[1]))
    y_sample = _trunk(x_sample, mem_sample, layers, _config(x_sample.shape[1]))
    return (y_prompt, y_sample)
```

```python
import functools

import numpy as np
import jax
import jax.numpy as jnp
from jax import lax
from jax.experimental import pallas as pl
from jax.experimental.pallas import tpu as pltpu

GRID_W = 64
ROPE_THETA = 10000.0
EPS = 1e-6
LOG2E = 1.4426950408889634

A_HEADS, A_KV_HEADS, A_HEAD_DIM = 8, 2, 64
B_HEADS, B_NOPE, B_ROPE, B_V = 8, 64, 32, 64
B_QK = B_NOPE + B_ROPE
B_Q_LORA, B_KV_LORA = 256, 128
M_HEADS, M_HEAD_DIM = 4, 128
N_GROUPS, EXPERTS_PER_GROUP, TOP_K = 4, 8, 2
N_EXPERTS = N_GROUPS * EXPERTS_PER_GROUP
LANES = 128
SUM_ROWS = 16
ISSUE_UNROLL = 8

VMEM_LIMIT = 56 * 1024 * 1024

BF16 = jnp.bfloat16
F32 = jnp.float32


def _cparams(sem, flags=None):
    return pltpu.CompilerParams(dimension_semantics=sem, vmem_limit_bytes=VMEM_LIMIT, flags=flags)


def _nt(a, b):
    return lax.dot_general(a, b, (((1,), (1,)), ((), ())), preferred_element_type=F32)


def _tn(a, b):
    return lax.dot_general(a, b, (((0,), (0,)), ((), ())), preferred_element_type=F32)


def _nn(a, b):
    return jnp.dot(a, b, preferred_element_type=F32)


def _rope_angles(seq_len, rot_dim):
    rows = seq_len // GRID_W
    r = jnp.repeat(jnp.arange(rows, dtype=F32), GRID_W)
    c = jnp.tile(jnp.arange(GRID_W, dtype=F32), rows)
    axis_dim = rot_dim // 2
    inv = ROPE_THETA ** (-(jnp.arange(axis_dim // 2, dtype=F32) * 2.0) / axis_dim)
    return jnp.concatenate([r[:, None] * inv, c[:, None] * inv], axis=-1)


def _rope_tables(seq_len):
    ang_a = _rope_angles(seq_len, A_HEAD_DIM)
    ang_b = _rope_angles(seq_len, B_ROPE)
    ca, sa = jnp.cos(ang_a), jnp.sin(ang_a)
    cb, sb = jnp.cos(ang_b), jnp.sin(ang_b)
    z64 = jnp.zeros((seq_len, 64), F32)
    z32 = jnp.zeros((seq_len, 32), F32)
    one64 = jnp.ones((seq_len, 64), F32)
    return dict(
        cos_a_t=ca.T, sin_a_t=sa.T,
        cos_b_t=cb.T, sin_b_t=sb.T,
        cos_ka=jnp.concatenate([ca, ca, z64], -1),
        sin_ka=jnp.concatenate([-sa, sa, z64], -1),
        cos_kb=jnp.concatenate([one64, cb, cb, z32], -1),
        sin_kb=jnp.concatenate([z64, -sb, sb, z32], -1),
    )


_EVEN64, _ODD64 = np.arange(0, 64, 2), np.arange(1, 64, 2)
_PERM64 = np.concatenate([_EVEN64, _ODD64])
_SWAP64 = np.concatenate([_ODD64, _EVEN64])
_EVEN32, _ODD32 = np.arange(0, 32, 2), np.arange(1, 32, 2)
_PERM32 = np.concatenate([_EVEN32, _ODD32])
_SWAP32 = np.concatenate([_ODD32, _EVEN32])


def _prep_layer(p, i):
    d_model = p['w_in'].shape[1]
    w_in = p['w_in'][i]
    o_ak = A_HEADS * A_HEAD_DIM
    o_av = o_ak + A_KV_HEADS * A_HEAD_DIM
    o_cq = o_av + A_KV_HEADS * A_HEAD_DIM
    o_ckv = o_cq + B_Q_LORA
    o_kr = o_ckv + B_KV_LORA
    zc64 = jnp.zeros((d_model, 64), F32)
    zc32 = jnp.zeros((d_model, 32), F32)

    aq_cols = np.concatenate([h * A_HEAD_DIM + _PERM64 for h in range(A_HEADS)])
    w_d = jnp.concatenate([w_in[:, aq_cols], w_in[:, o_av:o_cq], w_in[:, o_cq:o_ckv], w_in[:, o_ckv:o_kr]], -1)
    ak, akp = [], []
    for j in range(A_KV_HEADS):
        ak += [w_in[:, o_ak + j * A_HEAD_DIM + _PERM64], zc64]
        akp += [w_in[:, o_ak + j * A_HEAD_DIM + _SWAP64], zc64]
    kr = [zc64, w_in[:, o_kr + _PERM32], zc32]
    krp = [zc64, w_in[:, o_kr + _SWAP32], zc32]
    w_t = jnp.concatenate(ak + akp + [w_in[:, o_ckv:o_kr]] + kr + krp, -1)

    z64 = jnp.zeros((64,), F32)
    z32 = jnp.zeros((32,), F32)
    a_k = p['a_k_norm'][i]
    b_q = p['b_q_norm'][i]
    b_k = p['b_k_norm'][i]
    w_uq = p['b_w_uq'][i]
    uq_cols = np.concatenate([np.concatenate([h * B_QK + np.arange(B_NOPE), h * B_QK + B_NOPE + _PERM32])
                              for h in range(B_HEADS)])
    w_ukv = p['b_w_ukv'][i]
    zk = jnp.zeros((B_KV_LORA, 64), F32)
    wk_pad = jnp.concatenate(
        sum([[w_ukv[:, h * (B_NOPE + B_V):h * (B_NOPE + B_V) + B_NOPE], zk] for h in range(B_HEADS)], []), -1)
    wv = jnp.concatenate(
        [w_ukv[:, h * (B_NOPE + B_V) + B_NOPE:(h + 1) * (B_NOPE + B_V)] for h in range(B_HEADS)], -1)

    r_w = jnp.concatenate([p['r_coarse_w'][i], jnp.zeros((d_model, LANES - N_GROUPS), F32),
                           p['r_fine_w'][i], jnp.zeros((d_model, LANES - N_EXPERTS), F32)], -1)
    r_hi = r_w.astype(BF16)
    r_lo = (r_w - r_hi.astype(F32)).astype(BF16)
    r_b = jnp.concatenate([p['r_coarse_b'][i], jnp.zeros((LANES - N_GROUPS,), F32),
                           p['r_fine_b'][i], jnp.zeros((LANES - N_EXPERTS,), F32)])[None, :]
    return dict(
        g_mix=p['norm_mix'][i][None, :],
        w_d_t=w_d.T.astype(BF16),
        w_t=w_t.astype(BF16),
        g_aq=p['a_q_norm'][i][_PERM64][:, None],
        g_ak=jnp.concatenate([a_k[_PERM64], z64])[None, :],
        g_akp=jnp.concatenate([a_k[_SWAP64], z64])[None, :],
        g_cq=p['b_cq_norm'][i][:, None],
        g_ckv_row=p['b_ckv_norm'][i][None, :],
        g_ckv_col=p['b_ckv_norm'][i][:, None],
        w_uq_t=w_uq[:, uq_cols].T.astype(BF16),
        g_bq=jnp.concatenate([b_q[:B_NOPE], b_q[B_NOPE + _PERM32]])[:, None],
        wk_pad=wk_pad.astype(BF16),
        wv_t=wv.T.astype(BF16),
        g_bk=jnp.concatenate([b_k[:B_NOPE], b_k[B_NOPE + _PERM32], z32])[None, :],
        g_bkp=jnp.concatenate([z64, b_k[B_NOPE + _SWAP32], z32])[None, :],
        w_out=p['w_out'][i].astype(BF16),
        g_memx=p['norm_mem_x'][i][None, :],
        g_memm=p['norm_mem_m'][i][None, :],
        m_w_q=p['m_w_q'][i].astype(BF16),
        m_w_kv=p['m_w_kv'][i].astype(BF16),
        g_mq=p['m_q_norm'][i][None, :],
        g_mk=p['m_k_norm'][i][None, :],
        m_w_o=p['m_w_o'][i].astype(BF16),
        g_ffn=p['norm_ffn'][i][None, :],
        r_hi=r_hi, r_lo=r_lo, r_b=r_b,
        w_gu=jnp.concatenate([p['e_w_gate'][i], p['e_w_up'][i]], -1).astype(BF16),
        w_dn=p['e_w_down'][i].astype(BF16),
    )


def _in_proj_kernel(x_ref, g_mix, w_d_t, w_t, g_aq, g_ak, g_akp, g_cq, g_ckv_row, g_ckv_col, w_uq_t, g_bq,
                    wk_pad, wv_t, g_bk, g_bkp, cos_a_t, sin_a_t, cos_b_t, sin_b_t, cos_ka, sin_ka, cos_kb, sin_kb,
                    qa_ref, ka_ref, va_ref, qb_ref, kb_ref, vb_ref):
    x = x_ref[0]
    h = (x * lax.rsqrt(jnp.mean(x * x, axis=-1, keepdims=True) + EPS) * g_mix[...]).astype(BF16)
    pd = _nt(w_d_t[...], h)
    pt = _nn(h, w_t[...])
    ts = x.shape[0]

    ca, sa = cos_a_t[...], sin_a_t[...]
    qscale = A_HEAD_DIM ** -0.5 * LOG2E
    for hd in range(A_HEADS):
        q = pd[hd * 64:(hd + 1) * 64]
        qn = q * lax.rsqrt(jnp.mean(q * q, axis=0, keepdims=True) + EPS) * g_aq[...]
        x1, x2 = qn[:32], qn[32:]
        rot = jnp.concatenate([x1 * ca - x2 * sa, x1 * sa + x2 * ca], axis=0) * qscale
        qa_ref[0, hd] = rot.astype(BF16)
    o_av = A_HEADS * A_HEAD_DIM
    for j in range(A_KV_HEADS):
        va_ref[0, j] = pd[o_av + j * 64:o_av + (j + 1) * 64].astype(BF16)
    for j in range(A_KV_HEADS):
        k = pt[:, j * LANES:(j + 1) * LANES]
        kp = pt[:, (A_KV_HEADS + j) * LANES:(A_KV_HEADS + j + 1) * LANES]
        rs = lax.rsqrt(jnp.sum(k * k, axis=-1, keepdims=True) * (1.0 / A_HEAD_DIM) + EPS)
        ka_ref[0, j] = (rs * (k * g_ak[...] * cos_ka[...] + kp * g_akp[...] * sin_ka[...])).astype(BF16)

    o_cq = o_av + A_KV_HEADS * A_HEAD_DIM
    cq = pd[o_cq:o_cq + B_Q_LORA]
    cqn = (cq * lax.rsqrt(jnp.mean(cq * cq, axis=0, keepdims=True) + EPS) * g_cq[...]).astype(BF16)
    bq = _nn(w_uq_t[...], cqn)
    cb, sb = cos_b_t[...], sin_b_t[...]
    bscale = B_QK ** -0.5 * LOG2E
    zpad = jnp.zeros((LANES - B_QK, ts), F32)
    for hd in range(B_HEADS):
        q = bq[hd * B_QK:(hd + 1) * B_QK]
        qn = q * lax.rsqrt(jnp.mean(q * q, axis=0, keepdims=True) + EPS) * g_bq[...]
        x1, x2 = qn[B_NOPE:B_NOPE + 16], qn[B_NOPE + 16:]
        rot = jnp.concatenate([qn[:B_NOPE], x1 * cb - x2 * sb, x1 * sb + x2 * cb], axis=0) * bscale
        qb_ref[0, hd] = jnp.concatenate([rot, zpad], axis=0).astype(BF16)

    o_ckv_t = 2 * A_KV_HEADS * LANES
    ckv = pt[:, o_ckv_t:o_ckv_t + B_KV_LORA]
    ckvn = (ckv * lax.rsqrt(jnp.mean(ckv * ckv, axis=-1, keepdims=True) + EPS) * g_ckv_row[...]).astype(BF16)
    k_nope = _nn(ckvn, wk_pad[...])
    kr = pt[:, o_ckv_t + LANES:o_ckv_t + 2 * LANES]
    krp = pt[:, o_ckv_t + 2 * LANES:o_ckv_t + 3 * LANES]
    rope_part = krp * g_bkp[...] * sin_kb[...]
    for hd in range(B_HEADS):
        k = k_nope[:, hd * LANES:(hd + 1) * LANES] + kr
        rs = lax.rsqrt(jnp.sum(k * k, axis=-1, keepdims=True) * (1.0 / B_QK) + EPS)
        kb_ref[0, hd] = (rs * (k * g_bk[...] * cos_kb[...] + rope_part)).astype(BF16)
    o_ckv = o_cq + B_Q_LORA
    ckv_t = pd[o_ckv:o_ckv + B_KV_LORA]
    ckvn_t = (ckv_t * lax.rsqrt(jnp.mean(ckv_t * ckv_t, axis=0, keepdims=True) + EPS) * g_ckv_col[...]).astype(BF16)
    bv = _nn(wv_t[...], ckvn_t)
    for hd in range(B_HEADS):
        vb_ref[0, hd] = bv[hd * B_V:(hd + 1) * B_V].astype(BF16)


def _in_proj(x, lw, rt, ts):
    bsz, seq, d_model = x.shape
    grid = (bsz, seq // ts)
    full = lambda a: pl.BlockSpec(a.shape, lambda b, j: (0,) * a.ndim)
    weights = [lw[k] for k in ('g_mix', 'w_d_t', 'w_t', 'g_aq', 'g_ak', 'g_akp', 'g_cq', 'g_ckv_row', 'g_ckv_col',
                               'w_uq_t', 'g_bq', 'wk_pad', 'wv_t', 'g_bk', 'g_bkp')]
    tabs_t = [rt[k] for k in ('cos_a_t', 'sin_a_t', 'cos_b_t', 'sin_b_t')]
    tabs_k = [rt[k] for k in ('cos_ka', 'sin_ka', 'cos_kb', 'sin_kb')]
    in_specs = ([pl.BlockSpec((1, ts, d_model), lambda b, j: (b, j, 0))] + [full(w) for w in weights]
                + [pl.BlockSpec((t.shape[0], ts), lambda b, j: (0, j)) for t in tabs_t]
                + [pl.BlockSpec((ts, LANES), lambda b, j: (j, 0)) for _ in tabs_k])
    t_major = lambda nh: pl.BlockSpec((1, nh, ts, LANES), lambda b, j: (b, 0, j, 0))
    d_major = lambda nh, d: pl.BlockSpec((1, nh, d, ts), lambda b, j: (b, 0, 0, j))
    out_shape = [
        jax.ShapeDtypeStruct((bsz, A_HEADS, A_HEAD_DIM, seq), BF16),
        jax.ShapeDtypeStruct((bsz, A_KV_HEADS, seq, LANES), BF16),
        jax.ShapeDtypeStruct((bsz, A_KV_HEADS, A_HEAD_DIM, seq), BF16),
        jax.ShapeDtypeStruct((bsz, B_HEADS, LANES, seq), BF16),
        jax.ShapeDtypeStruct((bsz, B_HEADS, seq, LANES), BF16),
        jax.ShapeDtypeStruct((bsz, B_HEADS, B_V, seq), BF16),
    ]
    out_specs = [d_major(A_HEADS, A_HEAD_DIM), t_major(A_KV_HEADS), d_major(A_KV_HEADS, A_HEAD_DIM),
                 d_major(B_HEADS, LANES), t_major(B_HEADS), d_major(B_HEADS, B_V)]
    return pl.pallas_call(
        _in_proj_kernel, grid=grid, in_specs=in_specs, out_specs=out_specs, out_shape=out_shape,
        compiler_params=_cparams(("parallel", "parallel")), name="in_proj",
    )(x, *weights, *tabs_t, *tabs_k)


def _attn_kernel(q_ref, k_ref, v_ref, o_ref, s_scr, cmax_scr, *, dq, tk, wp, interleave, unroll):
    n_heads, _, tqb = q_ref.shape[1:]
    seq = k_ref.shape[2]
    dv = v_ref.shape[2]
    panels = [(g, pnl * wp) for g in range(n_heads) for pnl in range(tqb // wp)]
    n_chunks = seq // tk

    ones_rows = jnp.ones((SUM_ROWS, tk), BF16)

    def k_chunk(c):
        return k_ref[0, 0, pl.ds(pl.multiple_of(c * tk, tk), tk), :][:, :dq]

    def v_chunk(c):
        return jnp.concatenate([v_ref[0, 0, :, pl.ds(pl.multiple_of(c * tk, tk), tk)], ones_rows], axis=0)

    def score_panel(k, slot, n):
        g, lo = panels[n]
        s = _nn(k, q_ref[0, g, :, lo:lo + wp])
        s_scr[slot, n] = s
        cmax_scr[slot, n] = jnp.max(s, axis=0, keepdims=True)

    def update_panel(v, slot, n, m, acc):
        m_new = jnp.maximum(m, cmax_scr[slot, n])
        p = jnp.exp2((s_scr[slot, n] - m_new).astype(BF16))
        return m_new, jnp.exp2(m - m_new) * acc + _nn(v, p)

    def step(c_next, c_cur, slot_cur, state):
        k = None if c_next is None else k_chunk(c_next)
        v = v_chunk(c_cur)
        if k is not None and not interleave:
            for n in range(len(panels)):
                score_panel(k, 1 - slot_cur, n)
        out = []
        for n, (m, acc) in enumerate(state):
            if k is not None and interleave:
                score_panel(k, 1 - slot_cur, n)
            out.append(update_panel(v, slot_cur, n, m, acc))
        return tuple(out)

    def body(j, state):
        c = 2 * j
        state = step(c + 1, c, 0, state)
        return step(c + 2, c + 1, 1, state)

    state = tuple((jnp.full((1, wp), -jnp.inf, F32), jnp.zeros((dv + SUM_ROWS, wp), F32)) for _ in panels)
    k0 = k_chunk(0)
    for n in range(len(panels)):
        score_panel(k0, 0, n)
    state = lax.fori_loop(0, n_chunks // 2 - 1, body, state, unroll=unroll)
    state = step(n_chunks - 1, n_chunks - 2, 0, state)
    state = step(None, n_chunks - 1, 1, state)
    for (g, lo), (_, acc) in zip(panels, state):
        o_ref[0, g, :, lo:lo + wp] = (acc[:dv] * (1.0 / acc[dv:dv + 1])).astype(BF16)


def _attention(q_t, k, v_t, *, dq, tqb, tk, wp, interleave, unroll, name):
    bsz, hq, dq_rows, seq = q_t.shape
    hk = k.shape[1]
    grp = hq // hk
    dv = v_t.shape[2]
    grid = (bsz, hk, seq // tqb)
    return pl.pallas_call(
        functools.partial(_attn_kernel, dq=dq, tk=tk, wp=wp, interleave=interleave, unroll=unroll),
        grid=grid,
        in_specs=[pl.BlockSpec((1, grp, dq_rows, tqb), lambda b, j, i: (b, j, 0, i)),
                  pl.BlockSpec((1, 1, seq, LANES), lambda b, j, i: (b, j, 0, 0)),
                  pl.BlockSpec((1, 1, dv, seq), lambda b, j, i: (b, j, 0, 0))],
        out_specs=pl.BlockSpec((1, grp, dv, tqb), lambda b, j, i: (b, j, 0, i)),
        out_shape=jax.ShapeDtypeStruct((bsz, hq, dv, seq), BF16),
        scratch_shapes=[pltpu.VMEM((2, grp * (tqb // wp), tk, wp), F32),
                        pltpu.VMEM((2, grp * (tqb // wp), 1, wp), F32)],
        compiler_params=_cparams(("parallel", "parallel", "parallel")), name=name,
    )(q_t, k, v_t)


def _mem_kv_kernel(mem_ref, g_memm, w_kv, g_mk, k_ref, v_ref):
    m = mem_ref[0]
    mn = (m * lax.rsqrt(jnp.mean(m * m, axis=-1, keepdims=True) + EPS) * g_memm[...]).astype(BF16)
    kv = _nn(mn, w_kv[...])
    hw = M_HEADS * M_HEAD_DIM
    for hh in range(M_HEADS):
        k = kv[:, hh * M_HEAD_DIM:(hh + 1) * M_HEAD_DIM]
        kn = k * lax.rsqrt(jnp.mean(k * k, axis=-1, keepdims=True) + EPS) * g_mk[...]
        k_ref[0, :, hh * M_HEAD_DIM:(hh + 1) * M_HEAD_DIM] = kn.astype(BF16)
    v_ref[0] = kv[:, hw:].astype(BF16)


def _mem_kv(mem, lw):
    bsz, n_mem, d_model = mem.shape
    hw = M_HEADS * M_HEAD_DIM
    full = lambda a: pl.BlockSpec(a.shape, lambda b: (0,) * a.ndim)
    ws = [lw['g_memm'], lw['m_w_kv'], lw['g_mk']]
    return pl.pallas_call(
        _mem_kv_kernel, grid=(bsz,),
        in_specs=[pl.BlockSpec((1, n_mem, d_model), lambda b: (b, 0, 0))] + [full(w) for w in ws],
        out_specs=[pl.BlockSpec((1, n_mem, hw), lambda b: (b, 0, 0))] * 2,
        out_shape=[jax.ShapeDtypeStruct((bsz, n_mem, hw), BF16)] * 2,
        compiler_params=_cparams(("parallel",)), name="mem_kv",
    )(mem, *ws)


def _pack_bf16_pairs(lo, hi):
    lo_b = lax.bitcast_convert_type(lo.astype(BF16).astype(F32), jnp.uint32)
    hi_b = lax.bitcast_convert_type(hi.astype(BF16).astype(F32), jnp.uint32)
    return (hi_b & jnp.uint32(0xFFFF0000)) | (lo_b >> 16)


def _post_attn_kernel(x_ref, oa_ref, ob_ref, mk_ref, mv_ref, w_out, g_memx, m_w_q, g_mq, m_w_o, g_ffn,
                      r_hi, r_lo, r_b, x_out, h_out, route_out, cnt_out, cnt_scr):
    first = jnp.logical_and(pl.program_id(0) == 0, pl.program_id(1) == 0)

    @pl.when(first)
    def _():
        cnt_scr[...] = jnp.zeros_like(cnt_scr)

    ts = x_ref.shape[1]
    half = A_HEADS * A_HEAD_DIM
    oa = oa_ref[0].reshape(half, ts)
    ob = ob_ref[0].reshape(half, ts)
    x1 = x_ref[0] + _tn(oa, w_out[:half, :]) + _tn(ob, w_out[half:, :])

    h2 = (x1 * lax.rsqrt(jnp.mean(x1 * x1, axis=-1, keepdims=True) + EPS) * g_memx[...]).astype(BF16)
    qm = _nn(h2, m_w_q[...])
    mscale = M_HEAD_DIM ** -0.5 * LOG2E
    heads = []
    for hh in range(M_HEADS):
        sl = slice(hh * M_HEAD_DIM, (hh + 1) * M_HEAD_DIM)
        q = qm[:, sl]
        qn = (q * lax.rsqrt(jnp.mean(q * q, axis=-1, keepdims=True) + EPS) * g_mq[...] * mscale).astype(BF16)
        s = _nt(qn, mk_ref[0, :, sl])
        p = jnp.exp2(s - jnp.max(s, axis=-1, keepdims=True))
        o = _nn(p.astype(BF16), mv_ref[0, :, sl]) * (1.0 / jnp.sum(p, axis=-1, keepdims=True))
        heads.append(o.astype(BF16))
    x2 = x1 + _nn(jnp.concatenate(heads, axis=-1), m_w_o[...])
    x_out[0] = x2

    h3 = x2 * lax.rsqrt(jnp.mean(x2 * x2, axis=-1, keepdims=True) + EPS) * g_ffn[...]
    d_half = h3.shape[1] // 2
    h_out[0] = _pack_bf16_pairs(h3[:, :d_half], h3[:, d_half:])

    h_hi = h3.astype(BF16)
    h_lo = (h3 - h_hi.astype(F32)).astype(BF16)
    logits = _nn(h_hi, r_hi[...]) + _nn(h_hi, r_lo[...]) + _nn(h_lo, r_hi[...]) + r_b[...]
    lane = lax.broadcasted_iota(jnp.int32, (ts, LANES), 1)
    neg = jnp.float32(-jnp.inf)
    lc = jnp.where(lane < N_GROUPS, logits[:, :LANES], neg)
    mc = jnp.max(lc, axis=-1, keepdims=True)
    g_idx = jnp.min(jnp.where(lc == mc, lane, LANES), axis=-1, keepdims=True)
    g_w = 1.0 / jnp.sum(jnp.exp(lc - mc), axis=-1, keepdims=True)
    lf = jnp.where((lane >> 3) == g_idx, logits[:, LANES:], neg)
    m1 = jnp.max(lf, axis=-1, keepdims=True)
    i1 = jnp.min(jnp.where(lf == m1, lane, LANES), axis=-1, keepdims=True)
    lf2 = jnp.where(lane == i1, neg, lf)
    m2 = jnp.max(lf2, axis=-1, keepdims=True)
    i2 = jnp.min(jnp.where(lf2 == m2, lane, LANES), axis=-1, keepdims=True)
    r21 = jnp.exp(m2 - m1)
    gate1 = g_w / (1.0 + r21)
    gate2 = g_w * r21 / (1.0 + r21)

    hot1 = lane == i1
    hot2 = lane == i2
    onehot = jnp.where(jnp.logical_or(hot1, hot2), 1.0, 0.0)
    row = lax.broadcasted_iota(jnp.int32, (ts, ts), 0)
    col = lax.broadcasted_iota(jnp.int32, (ts, ts), 1)
    tri = jnp.where(col < row, 1.0, 0.0).astype(BF16)
    base = _nn(tri, onehot.astype(BF16)) + cnt_scr[...]
    rank1 = jnp.sum(jnp.where(hot1, base, 0.0), axis=-1, keepdims=True)
    rank2 = jnp.sum(jnp.where(hot2, base, 0.0), axis=-1, keepdims=True)
    cnt_scr[...] += jnp.sum(onehot, axis=0, keepdims=True)
    cnt_out[...] = cnt_scr[...]

    vals = (i1.astype(F32), i2.astype(F32), gate1, gate2, rank1, rank2)
    route = jnp.zeros((ts, LANES), F32)
    for n, v in enumerate(vals):
        route = jnp.where(lane == n, v, route)
    route_out[0] = route


def _post_attn(x, oa, ob, mk, mv, lw, ts):
    bsz, seq, d_model = x.shape
    n_mem, hw = mk.shape[1:]
    grid = (bsz, seq // ts)
    full = lambda a: pl.BlockSpec(a.shape, lambda b, j: (0,) * a.ndim)
    ws = [lw[k] for k in ('w_out', 'g_memx', 'm_w_q', 'g_mq', 'm_w_o', 'g_ffn', 'r_hi', 'r_lo', 'r_b')]
    tok = lambda w: pl.BlockSpec((1, ts, w), lambda b, j: (b, j, 0))
    return pl.pallas_call(
        _post_attn_kernel, grid=grid,
        in_specs=[tok(d_model),
                  pl.BlockSpec((1, A_HEADS, A_HEAD_DIM, ts), lambda b, j: (b, 0, 0, j)),
                  pl.BlockSpec((1, B_HEADS, B_V, ts), lambda b, j: (b, 0, 0, j)),
                  pl.BlockSpec((1, n_mem, hw), lambda b, j: (b, 0, 0)),
                  pl.BlockSpec((1, n_mem, hw), lambda b, j: (b, 0, 0))] + [full(w) for w in ws],
        out_specs=[tok(d_model), tok(d_model // 2), tok(LANES), pl.BlockSpec((1, LANES), lambda b, j: (0, 0))],
        out_shape=[jax.ShapeDtypeStruct((bsz, seq, d_model), F32),
                   jax.ShapeDtypeStruct((bsz, seq, d_model // 2), jnp.uint32),
                   jax.ShapeDtypeStruct((bsz, seq, LANES), F32),
                   jax.ShapeDtypeStruct((1, LANES), F32)],
        scratch_shapes=[pltpu.VMEM((1, LANES), F32)],
        compiler_params=_cparams(("arbitrary", "arbitrary")), name="post_attn",
    )(x, oa, ob, mk, mv, *ws)


def _row_copy(src, dst, sem, s_row, d_row):
    return pltpu.make_async_copy(src.at[pl.ds(s_row, 1)], dst.at[pl.ds(d_row, 1)], sem)


def _dispatch_kernel(dest_ref, h_ref, xs_in, xs_out, sem, *, td):
    del xs_in

    def issue(t, c):
        for kk in range(TOP_K):
            _row_copy(h_ref, xs_out, sem, t, dest_ref[0, 0, TOP_K * t + kk]).start(priority=kk)
        return c

    lax.fori_loop(0, td, issue, 0, unroll=ISSUE_UNROLL)
    for kk in range(TOP_K):
        pltpu.make_async_copy(h_ref, xs_out.at[pl.ds(0, td)], sem).wait()


def _dispatch(h_packed, dest, n_rows, td):
    n_tok, width = h_packed.shape
    dest3 = dest.reshape(n_tok // td, 1, TOP_K * td)
    xs0 = jnp.zeros((n_rows, width), h_packed.dtype)
    return pl.pallas_call(
        functools.partial(_dispatch_kernel, td=td), grid=(n_tok // td,),
        in_specs=[pl.BlockSpec((1, 1, TOP_K * td), lambda i: (i, 0, 0), memory_space=pltpu.SMEM),
                  pl.BlockSpec((td, width), lambda i: (i, 0)), pl.BlockSpec(memory_space=pl.ANY)],
        out_specs=pl.BlockSpec(memory_space=pl.ANY),
        out_shape=jax.ShapeDtypeStruct((n_rows, width), h_packed.dtype),
        scratch_shapes=[pltpu.SemaphoreType.DMA(())],
        input_output_aliases={2: 0},
        compiler_params=_cparams(("arbitrary",)), name="moe_dispatch",
    )(dest3, h_packed, xs0)


def _experts_kernel(be_ref, nu_ref, xs_ref, w_gu, w_dn, ys_ref):
    i = pl.program_id(0)

    @pl.when(i < nu_ref[0])
    def _():
        packed = xs_ref[...]
        lo = lax.bitcast_convert_type(packed << 16, F32)
        hi = lax.bitcast_convert_type(packed & jnp.uint32(0xFFFF0000), F32)
        xb = jnp.concatenate([lo, hi], axis=-1).astype(BF16)
        gu = _nn(xb, w_gu[0])
        d_e = gu.shape[1] // 2
        g = gu[:, :d_e]
        a = g * (1.0 / (1.0 + jnp.exp(-g))) * gu[:, d_e:]
        ys_ref[...] = _nn(a.astype(BF16), w_dn[0])

    @pl.when(i >= nu_ref[0])
    def _():
        ys_ref[...] = jnp.zeros_like(ys_ref)


def _experts(xs, block_expert, n_used, lw, bm):
    n_rows, width = xs.shape
    d_model = 2 * width
    w_gu, w_dn = lw['w_gu'], lw['w_dn']
    grid_spec = pltpu.PrefetchScalarGridSpec(
        num_scalar_prefetch=2, grid=(n_rows // bm,),
        in_specs=[pl.BlockSpec((bm, width), lambda i, be, nu: (i, 0)),
                  pl.BlockSpec((1,) + w_gu.shape[1:], lambda i, be, nu: (be[i], 0, 0)),
                  pl.BlockSpec((1,) + w_dn.shape[1:], lambda i, be, nu: (be[i], 0, 0))],
        out_specs=pl.BlockSpec((bm, d_model), lambda i, be, nu: (i, 0)))
    return pl.pallas_call(
        _experts_kernel, grid_spec=grid_spec,
        out_shape=jax.ShapeDtypeStruct((n_rows, d_model), F32),
        compiler_params=_cparams(("arbitrary",)), name="moe_experts",
    )(block_expert, n_used, xs, w_gu, w_dn)


def _combine_kernel(dest_ref, ys_ref, x_ref, route_ref, o_ref, buf, sem, *, tc):
    def issue(t, c):
        for kk in range(TOP_K):
            pltpu.make_async_copy(ys_ref.at[pl.ds(dest_ref[0, 0, TOP_K * t + kk], 1)],
                                  buf.at[kk, pl.ds(t, 1)], sem).start(priority=kk)
        return c

    lax.fori_loop(0, tc, issue, 0, unroll=ISSUE_UNROLL)
    for kk in range(TOP_K):
        pltpu.make_async_copy(ys_ref.at[pl.ds(0, tc)], buf.at[kk], sem).wait()
    route = route_ref[...]
    o_ref[...] = x_ref[...] + route[:, 2:3] * buf[0] + route[:, 3:4] * buf[1]


def _combine(ys, dest, x, route, tc):
    n_tok, d_model = x.shape
    dest3 = dest.reshape(n_tok // tc, 1, TOP_K * tc)
    return pl.pallas_call(
        functools.partial(_combine_kernel, tc=tc), grid=(n_tok // tc,),
        in_specs=[pl.BlockSpec((1, 1, TOP_K * tc), lambda i: (i, 0, 0), memory_space=pltpu.SMEM),
                  pl.BlockSpec(memory_space=pl.ANY),
                  pl.BlockSpec((tc, d_model), lambda i: (i, 0)),
                  pl.BlockSpec((tc, LANES), lambda i: (i, 0))],
        out_specs=pl.BlockSpec((tc, d_model), lambda i: (i, 0)),
        out_shape=jax.ShapeDtypeStruct((n_tok, d_model), F32),
        scratch_shapes=[pltpu.VMEM((TOP_K, tc, d_model), F32), pltpu.SemaphoreType.DMA(())],
        compiler_params=_cparams(("arbitrary",)), name="moe_combine",
    )(dest3, ys, x, route)


def _moe(x2, h_packed, route, counts, lw, cfg):
    bsz, seq, d_model = x2.shape
    n_tok = bsz * seq
    bm = cfg['bm']
    cnt = counts[0, :N_EXPERTS].astype(jnp.int32)
    padded = ((cnt + bm - 1) // bm) * bm
    ends = jnp.cumsum(padded)
    starts = ends - padded
    n_blk = -(-(n_tok * TOP_K + N_EXPERTS * (bm - 1)) // bm)
    route2 = route.reshape(n_tok, LANES)
    expert = route2[:, 0:TOP_K].astype(jnp.int32)
    rank = route2[:, 4:4 + TOP_K].astype(jnp.int32)
    e_ids = jnp.arange(N_EXPERTS, dtype=jnp.int32)
    start_of = jnp.sum(jnp.where(expert[..., None] == e_ids, starts, 0), axis=-1)
    dest = (start_of + rank).reshape(-1)
    blk_row = jnp.arange(n_blk, dtype=jnp.int32) * bm
    block_expert = jnp.minimum(jnp.sum((ends[None, :] <= blk_row[:, None]).astype(jnp.int32), axis=-1), N_EXPERTS - 1)
    n_used = (ends[-1:] // bm).astype(jnp.int32)
    xs = _dispatch(h_packed.reshape(n_tok, d_model // 2), dest, n_blk * bm, cfg['td'])
    ys = _experts(xs, block_expert, n_used, lw, bm)
    out = _combine(ys, dest, x2.reshape(n_tok, d_model), route2, cfg['tc'])
    return out.reshape(bsz, seq, d_model)


def _config(seq):
    if seq % 2048 == 0:
        tk, unroll = (512, 4) if seq >= 8192 else (256, 2)
        return dict(ts=512, tq_a=512, tq_b=2048, tk=tk, wp=256, interleave=True, unroll=unroll,
                    bm=512, td=1024, tc=1024)
    return dict(ts=128, tq_a=128, tq_b=256, tk=64, wp=128, interleave=True, unroll=1,
                bm=64, td=64, tc=64)


def _trunk(x, mem, layers, cfg):
    rt = _rope_tables(x.shape[1])
    for lw in layers:
        qa, ka, va, qb, kb, vb = _in_proj(x, lw, rt, cfg['ts'])
        att = dict(tk=cfg['tk'], wp=cfg['wp'], interleave=cfg['interleave'], unroll=cfg['unroll'])
        oa = _attention(qa, ka, va, dq=A_HEAD_DIM, tqb=cfg['tq_a'], name="attn_a", **att)
        ob = _attention(qb, kb, vb, dq=LANES, tqb=cfg['tq_b'], name="attn_b", **att)
        mk, mv = _mem_kv(mem, lw)
        x2, h_packed, route, counts = _post_attn(x, oa, ob, mk, mv, lw, cfg['ts'])
        x = _moe(x2, h_packed, route, counts, lw, cfg)
    return x


def kernel(x_prompt, x_sample, mem_prompt, mem_sample, norm_mix, w_in, a_q_norm, a_k_norm, b_cq_norm, b_ckv_norm, b_w_uq, b_w_ukv, b_q_norm, b_k_norm, w_out, norm_mem_x, norm_mem_m, m_w_q, m_w_kv, m_q_norm, m_k_norm, m_w_o, norm_ffn, r_coarse_w, r_coarse_b, r_fine_w, r_fine_b, e_w_gate, e_w_up, e_w_down):
    p = dict(norm_mix=norm_mix, w_in=w_in, a_q_norm=a_q_norm, a_k_norm=a_k_norm, b_cq_norm=b_cq_norm,
             b_ckv_norm=b_ckv_norm, b_w_uq=b_w_uq, b_w_ukv=b_w_ukv, b_q_norm=b_q_norm, b_k_norm=b_k_norm,
             w_out=w_out, norm_mem_x=norm_mem_x, norm_mem_m=norm_mem_m, m_w_q=m_w_q, m_w_kv=m_w_kv,
             m_q_norm=m_q_norm, m_k_norm=m_k_norm, m_w_o=m_w_o, norm_ffn=norm_ffn, r_coarse_w=r_coarse_w,
             r_coarse_b=r_coarse_b, r_fine_w=r_fine_w, r_fine_b=r_fine_b, e_w_gate=e_w_gate, e_w_up=e_w_up,
             e_w_down=e_w_down)
    layers = [_prep_layer(p, i) for i in range(w_in.shape[0])]
    y_prompt = _trunk(x_prompt, mem_prompt, layers, _config(x_prompt.shape[1]))
    y_sample = _trunk(x_sample, mem_sample, layers, _config(x_sample.shape[1]))
    return (y_prompt, y_sample)
```

```python
import functools

import numpy as np
import jax
import jax.numpy as jnp
from jax import lax
from jax.experimental import pallas as pl
from jax.experimental.pallas import tpu as pltpu

GRID_W = 64
ROPE_THETA = 10000.0
EPS = 1e-6
LOG2E = 1.4426950408889634

A_HEADS, A_KV_HEADS, A_HEAD_DIM = 8, 2, 64
B_HEADS, B_NOPE, B_ROPE, B_V = 8, 64, 32, 64
B_QK = B_NOPE + B_ROPE
B_Q_LORA, B_KV_LORA = 256, 128
M_HEADS, M_HEAD_DIM = 4, 128
N_GROUPS, EXPERTS_PER_GROUP, TOP_K = 4, 8, 2
N_EXPERTS = N_GROUPS * EXPERTS_PER_GROUP
LANES = 128
SUM_ROWS = 16
ISSUE_UNROLL = 8

VMEM_LIMIT = 56 * 1024 * 1024

BF16 = jnp.bfloat16
F32 = jnp.float32


def _cparams(sem, flags=None):
    return pltpu.CompilerParams(dimension_semantics=sem, vmem_limit_bytes=VMEM_LIMIT, flags=flags)


def _nt(a, b):
    return lax.dot_general(a, b, (((1,), (1,)), ((), ())), preferred_element_type=F32)


def _tn(a, b):
    return lax.dot_general(a, b, (((0,), (0,)), ((), ())), preferred_element_type=F32)


def _nn(a, b):
    return jnp.dot(a, b, preferred_element_type=F32)


def _rope_angles(seq_len, rot_dim):
    rows = seq_len // GRID_W
    r = jnp.repeat(jnp.arange(rows, dtype=F32), GRID_W)
    c = jnp.tile(jnp.arange(GRID_W, dtype=F32), rows)
    axis_dim = rot_dim // 2
    inv = ROPE_THETA ** (-(jnp.arange(axis_dim // 2, dtype=F32) * 2.0) / axis_dim)
    return jnp.concatenate([r[:, None] * inv, c[:, None] * inv], axis=-1)


def _rope_tables(seq_len):
    ang_a = _rope_angles(seq_len, A_HEAD_DIM)
    ang_b = _rope_angles(seq_len, B_ROPE)
    ca, sa = jnp.cos(ang_a), jnp.sin(ang_a)
    cb, sb = jnp.cos(ang_b), jnp.sin(ang_b)
    z64 = jnp.zeros((seq_len, 64), F32)
    z32 = jnp.zeros((seq_len, 32), F32)
    one64 = jnp.ones((seq_len, 64), F32)
    return dict(
        cos_a_t=ca.T, sin_a_t=sa.T,
        cos_b_t=cb.T, sin_b_t=sb.T,
        cos_ka=jnp.concatenate([ca, ca, z64], -1),
        sin_ka=jnp.concatenate([-sa, sa, z64], -1),
        cos_kb=jnp.concatenate([one64, cb, cb, z32], -1),
        sin_kb=jnp.concatenate([z64, -sb, sb, z32], -1),
    )


_EVEN64, _ODD64 = np.arange(0, 64, 2), np.arange(1, 64, 2)
_PERM64 = np.concatenate([_EVEN64, _ODD64])
_SWAP64 = np.concatenate([_ODD64, _EVEN64])
_EVEN32, _ODD32 = np.arange(0, 32, 2), np.arange(1, 32, 2)
_PERM32 = np.concatenate([_EVEN32, _ODD32])
_SWAP32 = np.concatenate([_ODD32, _EVEN32])


def _prep_layer(p, i):
    d_model = p['w_in'].shape[1]
    w_in = p['w_in'][i]
    o_ak = A_HEADS * A_HEAD_DIM
    o_av = o_ak + A_KV_HEADS * A_HEAD_DIM
    o_cq = o_av + A_KV_HEADS * A_HEAD_DIM
    o_ckv = o_cq + B_Q_LORA
    o_kr = o_ckv + B_KV_LORA
    zc64 = jnp.zeros((d_model, 64), F32)
    zc32 = jnp.zeros((d_model, 32), F32)

    aq_cols = np.concatenate([h * A_HEAD_DIM + _PERM64 for h in range(A_HEADS)])
    w_d = jnp.concatenate([w_in[:, aq_cols], w_in[:, o_av:o_cq], w_in[:, o_cq:o_ckv], w_in[:, o_ckv:o_kr]], -1)
    ak, akp = [], []
    for j in range(A_KV_HEADS):
        ak += [w_in[:, o_ak + j * A_HEAD_DIM + _PERM64], zc64]
        akp += [w_in[:, o_ak + j * A_HEAD_DIM + _SWAP64], zc64]
    kr = [zc64, w_in[:, o_kr + _PERM32], zc32]
    krp = [zc64, w_in[:, o_kr + _SWAP32], zc32]
    w_t = jnp.concatenate(ak + akp + [w_in[:, o_ckv:o_kr]] + kr + krp, -1)

    z64 = jnp.zeros((64,), F32)
    z32 = jnp.zeros((32,), F32)
    a_k = p['a_k_norm'][i]
    b_q = p['b_q_norm'][i]
    b_k = p['b_k_norm'][i]
    w_uq = p['b_w_uq'][i]
    uq_cols = np.concatenate([np.concatenate([h * B_QK + np.arange(B_NOPE), h * B_QK + B_NOPE + _PERM32])
                              for h in range(B_HEADS)])
    w_ukv = p['b_w_ukv'][i]
    zk = jnp.zeros((B_KV_LORA, 64), F32)
    wk_pad = jnp.concatenate(
        sum([[w_ukv[:, h * (B_NOPE + B_V):h * (B_NOPE + B_V) + B_NOPE], zk] for h in range(B_HEADS)], []), -1)
    wv = jnp.concatenate(
        [w_ukv[:, h * (B_NOPE + B_V) + B_NOPE:(h + 1) * (B_NOPE + B_V)] for h in range(B_HEADS)], -1)

    r_w = jnp.concatenate([p['r_coarse_w'][i], jnp.zeros((d_model, LANES - N_GROUPS), F32),
                           p['r_fine_w'][i], jnp.zeros((d_model, LANES - N_EXPERTS), F32)], -1)
    r_hi = r_w.astype(BF16)
    r_lo = (r_w - r_hi.astype(F32)).astype(BF16)
    r_b = jnp.concatenate([p['r_coarse_b'][i], jnp.zeros((LANES - N_GROUPS,), F32),
                           p['r_fine_b'][i], jnp.zeros((LANES - N_EXPERTS,), F32)])[None, :]
    return dict(
        g_mix=p['norm_mix'][i][None, :],
        w_d_t=w_d.T.astype(BF16),
        w_t=w_t.astype(BF16),
        g_aq=p['a_q_norm'][i][_PERM64][:, None],
        g_ak=jnp.concatenate([a_k[_PERM64], z64])[None, :],
        g_akp=jnp.concatenate([a_k[_SWAP64], z64])[None, :],
        g_cq=p['b_cq_norm'][i][:, None],
        g_ckv_row=p['b_ckv_norm'][i][None, :],
        g_ckv_col=p['b_ckv_norm'][i][:, None],
        w_uq_t=w_uq[:, uq_cols].T.astype(BF16),
        g_bq=jnp.concatenate([b_q[:B_NOPE], b_q[B_NOPE + _PERM32]])[:, None],
        wk_pad=wk_pad.astype(BF16),
        wv_t=wv.T.astype(BF16),
        g_bk=jnp.concatenate([b_k[:B_NOPE], b_k[B_NOPE + _PERM32], z32])[None, :],
        g_bkp=jnp.concatenate([z64, b_k[B_NOPE + _SWAP32], z32])[None, :],
        w_out=p['w_out'][i].astype(BF16),
        g_memx=p['norm_mem_x'][i][None, :],
        g_memm=p['norm_mem_m'][i][None, :],
        m_w_q=p['m_w_q'][i].astype(BF16),
        m_w_kv=p['m_w_kv'][i].astype(BF16),
        g_mq=p['m_q_norm'][i][None, :],
        g_mk=p['m_k_norm'][i][None, :],
        m_w_o=p['m_w_o'][i].astype(BF16),
        g_ffn=p['norm_ffn'][i][None, :],
        r_hi=r_hi, r_lo=r_lo, r_b=r_b,
        w_gu=jnp.concatenate([p['e_w_gate'][i], p['e_w_up'][i]], -1).astype(BF16),
        w_dn=p['e_w_down'][i].astype(BF16),
    )


def _in_proj_kernel(x_ref, g_mix, w_d_t, w_t, g_aq, g_ak, g_akp, g_cq, g_ckv_row, g_ckv_col, w_uq_t, g_bq,
                    wk_pad, wv_t, g_bk, g_bkp, cos_a_t, sin_a_t, cos_b_t, sin_b_t, cos_ka, sin_ka, cos_kb, sin_kb,
                    qa_ref, ka_ref, va_ref, qb_ref, kb_ref, vb_ref):
    x = x_ref[0]
    h = (x * lax.rsqrt(jnp.mean(x * x, axis=-1, keepdims=True) + EPS) * g_mix[...]).astype(BF16)
    pd = _nt(w_d_t[...], h)
    pt = _nn(h, w_t[...])
    ts = x.shape[0]

    ca, sa = cos_a_t[...], sin_a_t[...]
    qscale = A_HEAD_DIM ** -0.5 * LOG2E
    for hd in range(A_HEADS):
        q = pd[hd * 64:(hd + 1) * 64]
        qn = q * lax.rsqrt(jnp.mean(q * q, axis=0, keepdims=True) + EPS) * g_aq[...]
        x1, x2 = qn[:32], qn[32:]
        rot = jnp.concatenate([x1 * ca - x2 * sa, x1 * sa + x2 * ca], axis=0) * qscale
        qa_ref[0, hd] = rot.astype(BF16)
    o_av = A_HEADS * A_HEAD_DIM
    for j in range(A_KV_HEADS):
        va_ref[0, j] = pd[o_av + j * 64:o_av + (j + 1) * 64].astype(BF16)
    for j in range(A_KV_HEADS):
        k = pt[:, j * LANES:(j + 1) * LANES]
        kp = pt[:, (A_KV_HEADS + j) * LANES:(A_KV_HEADS + j + 1) * LANES]
        rs = lax.rsqrt(jnp.sum(k * k, axis=-1, keepdims=True) * (1.0 / A_HEAD_DIM) + EPS)
        ka_ref[0, j] = (rs * (k * g_ak[...] * cos_ka[...] + kp * g_akp[...] * sin_ka[...])).astype(BF16)

    o_cq = o_av + A_KV_HEADS * A_HEAD_DIM
    cq = pd[o_cq:o_cq + B_Q_LORA]
    cqn = (cq * lax.rsqrt(jnp.mean(cq * cq, axis=0, keepdims=True) + EPS) * g_cq[...]).astype(BF16)
    bq = _nn(w_uq_t[...], cqn)
    cb, sb = cos_b_t[...], sin_b_t[...]
    bscale = B_QK ** -0.5 * LOG2E
    for hd in range(B_HEADS):
        q = bq[hd * B_QK:(hd + 1) * B_QK]
        qn = q * lax.rsqrt(jnp.mean(q * q, axis=0, keepdims=True) + EPS) * g_bq[...]
        x1, x2 = qn[B_NOPE:B_NOPE + 16], qn[B_NOPE + 16:]
        rot = jnp.concatenate([qn[:B_NOPE], x1 * cb - x2 * sb, x1 * sb + x2 * cb], axis=0) * bscale
        qb_ref[0, hd] = rot.astype(BF16)

    o_ckv_t = 2 * A_KV_HEADS * LANES
    ckv = pt[:, o_ckv_t:o_ckv_t + B_KV_LORA]
    ckvn = (ckv * lax.rsqrt(jnp.mean(ckv * ckv, axis=-1, keepdims=True) + EPS) * g_ckv_row[...]).astype(BF16)
    k_nope = _nn(ckvn, wk_pad[...])
    kr = pt[:, o_ckv_t + LANES:o_ckv_t + 2 * LANES]
    krp = pt[:, o_ckv_t + 2 * LANES:o_ckv_t + 3 * LANES]
    rope_part = krp * g_bkp[...] * sin_kb[...]
    for hd in range(B_HEADS):
        k = k_nope[:, hd * LANES:(hd + 1) * LANES] + kr
        rs = lax.rsqrt(jnp.sum(k * k, axis=-1, keepdims=True) * (1.0 / B_QK) + EPS)
        kb_ref[0, hd] = (rs * (k * g_bk[...] * cos_kb[...] + rope_part)).astype(BF16)
    o_ckv = o_cq + B_Q_LORA
    ckv_t = pd[o_ckv:o_ckv + B_KV_LORA]
    ckvn_t = (ckv_t * lax.rsqrt(jnp.mean(ckv_t * ckv_t, axis=0, keepdims=True) + EPS) * g_ckv_col[...]).astype(BF16)
    bv = _nn(wv_t[...], ckvn_t)
    for hd in range(B_HEADS):
        vb_ref[0, hd] = bv[hd * B_V:(hd + 1) * B_V].astype(BF16)


def _in_proj(x, lw, rt, ts):
    bsz, seq, d_model = x.shape
    grid = (bsz, seq // ts)
    full = lambda a: pl.BlockSpec(a.shape, lambda b, j: (0,) * a.ndim)
    weights = [lw[k] for k in ('g_mix', 'w_d_t', 'w_t', 'g_aq', 'g_ak', 'g_akp', 'g_cq', 'g_ckv_row', 'g_ckv_col',
                               'w_uq_t', 'g_bq', 'wk_pad', 'wv_t', 'g_bk', 'g_bkp')]
    tabs_t = [rt[k] for k in ('cos_a_t', 'sin_a_t', 'cos_b_t', 'sin_b_t')]
    tabs_k = [rt[k] for k in ('cos_ka', 'sin_ka', 'cos_kb', 'sin_kb')]
    in_specs = ([pl.BlockSpec((1, ts, d_model), lambda b, j: (b, j, 0))] + [full(w) for w in weights]
                + [pl.BlockSpec((t.shape[0], ts), lambda b, j: (0, j)) for t in tabs_t]
                + [pl.BlockSpec((ts, LANES), lambda b, j: (j, 0)) for _ in tabs_k])
    t_major = lambda nh: pl.BlockSpec((1, nh, ts, LANES), lambda b, j: (b, 0, j, 0))
    d_major = lambda nh, d: pl.BlockSpec((1, nh, d, ts), lambda b, j: (b, 0, 0, j))
    out_shape = [
        jax.ShapeDtypeStruct((bsz, A_HEADS, A_HEAD_DIM, seq), BF16),
        jax.ShapeDtypeStruct((bsz, A_KV_HEADS, seq, LANES), BF16),
        jax.ShapeDtypeStruct((bsz, A_KV_HEADS, A_HEAD_DIM, seq), BF16),
        jax.ShapeDtypeStruct((bsz, B_HEADS, B_QK, seq), BF16),
        jax.ShapeDtypeStruct((bsz, B_HEADS, seq, LANES), BF16),
        jax.ShapeDtypeStruct((bsz, B_HEADS, B_V, seq), BF16),
    ]
    out_specs = [d_major(A_HEADS, A_HEAD_DIM), t_major(A_KV_HEADS), d_major(A_KV_HEADS, A_HEAD_DIM),
                 d_major(B_HEADS, B_QK), t_major(B_HEADS), d_major(B_HEADS, B_V)]
    return pl.pallas_call(
        _in_proj_kernel, grid=grid, in_specs=in_specs, out_specs=out_specs, out_shape=out_shape,
        compiler_params=_cparams(("parallel", "parallel")), name="in_proj",
    )(x, *weights, *tabs_t, *tabs_k)


def _attn_kernel(q_ref, k_ref, v_ref, o_ref, s_scr, cmax_scr, *, dq, tk, wp, interleave, unroll):
    n_heads, _, tqb = q_ref.shape[1:]
    seq = k_ref.shape[2]
    dv = v_ref.shape[2]
    panels = [(g, pnl * wp) for g in range(n_heads) for pnl in range(tqb // wp)]
    n_chunks = seq // tk

    ones_rows = jnp.ones((SUM_ROWS, tk), BF16)

    def k_chunk(c):
        return k_ref[0, 0, pl.ds(pl.multiple_of(c * tk, tk), tk), :][:, :dq]

    def v_chunk(c):
        return jnp.concatenate([v_ref[0, 0, :, pl.ds(pl.multiple_of(c * tk, tk), tk)], ones_rows], axis=0)

    def score_panel(k, slot, n):
        g, lo = panels[n]
        s = _nn(k, q_ref[0, g, :, lo:lo + wp])
        s_scr[slot, n] = s
        cmax_scr[slot, n] = jnp.max(s, axis=0, keepdims=True)

    def update_panel(v, slot, n, m, acc):
        m_new = jnp.maximum(m, cmax_scr[slot, n])
        p = jnp.exp2((s_scr[slot, n] - m_new).astype(BF16))
        return m_new, jnp.exp2(m - m_new) * acc + _nn(v, p)

    def step(c_next, c_cur, slot_cur, state):
        k = None if c_next is None else k_chunk(c_next)
        v = v_chunk(c_cur)
        if k is not None and not interleave:
            for n in range(len(panels)):
                score_panel(k, 1 - slot_cur, n)
        out = []
        for n, (m, acc) in enumerate(state):
            if k is not None and interleave:
                score_panel(k, 1 - slot_cur, n)
            out.append(update_panel(v, slot_cur, n, m, acc))
        return tuple(out)

    def body(j, state):
        c = 2 * j
        state = step(c + 1, c, 0, state)
        return step(c + 2, c + 1, 1, state)

    state = tuple((jnp.full((1, wp), -jnp.inf, F32), jnp.zeros((dv + SUM_ROWS, wp), F32)) for _ in panels)
    k0 = k_chunk(0)
    for n in range(len(panels)):
        score_panel(k0, 0, n)
    state = lax.fori_loop(0, n_chunks // 2 - 1, body, state, unroll=unroll)
    state = step(n_chunks - 1, n_chunks - 2, 0, state)
    state = step(None, n_chunks - 1, 1, state)
    for (g, lo), (_, acc) in zip(panels, state):
        o_ref[0, g, :, lo:lo + wp] = (acc[:dv] * (1.0 / acc[dv:dv + 1])).astype(BF16)


def _attention(q_t, k, v_t, *, dq, tqb, tk, wp, interleave, unroll, name):
    bsz, hq, dq_rows, seq = q_t.shape
    hk = k.shape[1]
    grp = hq // hk
    dv = v_t.shape[2]
    grid = (bsz, hk, seq // tqb)
    return pl.pallas_call(
        functools.partial(_attn_kernel, dq=dq, tk=tk, wp=wp, interleave=interleave, unroll=unroll),
        grid=grid,
        in_specs=[pl.BlockSpec((1, grp, dq_rows, tqb), lambda b, j, i: (b, j, 0, i)),
                  pl.BlockSpec((1, 1, seq, LANES), lambda b, j, i: (b, j, 0, 0)),
                  pl.BlockSpec((1, 1, dv, seq), lambda b, j, i: (b, j, 0, 0))],
        out_specs=pl.BlockSpec((1, grp, dv, tqb), lambda b, j, i: (b, j, 0, i)),
        out_shape=jax.ShapeDtypeStruct((bsz, hq, dv, seq), BF16),
        scratch_shapes=[pltpu.VMEM((2, grp * (tqb // wp), tk, wp), F32),
                        pltpu.VMEM((2, grp * (tqb // wp), 1, wp), F32)],
        compiler_params=_cparams(("parallel", "parallel", "parallel")), name=name,
    )(q_t, k, v_t)


def _mem_kv_kernel(mem_ref, g_memm, w_kv, g_mk, k_ref, v_ref):
    m = mem_ref[0]
    mn = (m * lax.rsqrt(jnp.mean(m * m, axis=-1, keepdims=True) + EPS) * g_memm[...]).astype(BF16)
    kv = _nn(mn, w_kv[...])
    hw = M_HEADS * M_HEAD_DIM
    for hh in range(M_HEADS):
        k = kv[:, hh * M_HEAD_DIM:(hh + 1) * M_HEAD_DIM]
        kn = k * lax.rsqrt(jnp.mean(k * k, axis=-1, keepdims=True) + EPS) * g_mk[...]
        k_ref[0, :, hh * M_HEAD_DIM:(hh + 1) * M_HEAD_DIM] = kn.astype(BF16)
    v_ref[0] = kv[:, hw:].astype(BF16)


def _mem_kv(mem, lw):
    bsz, n_mem, d_model = mem.shape
    hw = M_HEADS * M_HEAD_DIM
    full = lambda a: pl.BlockSpec(a.shape, lambda b: (0,) * a.ndim)
    ws = [lw['g_memm'], lw['m_w_kv'], lw['g_mk']]
    return pl.pallas_call(
        _mem_kv_kernel, grid=(bsz,),
        in_specs=[pl.BlockSpec((1, n_mem, d_model), lambda b: (b, 0, 0))] + [full(w) for w in ws],
        out_specs=[pl.BlockSpec((1, n_mem, hw), lambda b: (b, 0, 0))] * 2,
        out_shape=[jax.ShapeDtypeStruct((bsz, n_mem, hw), BF16)] * 2,
        compiler_params=_cparams(("parallel",)), name="mem_kv",
    )(mem, *ws)


def _pack_bf16_pairs(lo, hi):
    lo_b = lax.bitcast_convert_type(lo.astype(BF16).astype(F32), jnp.uint32)
    hi_b = lax.bitcast_convert_type(hi.astype(BF16).astype(F32), jnp.uint32)
    return (hi_b & jnp.uint32(0xFFFF0000)) | (lo_b >> 16)


def _post_attn_kernel(x_ref, oa_ref, ob_ref, mk_ref, mv_ref, w_out, g_memx, m_w_q, g_mq, m_w_o, g_ffn,
                      r_hi, r_lo, r_b, x_out, h_out, route_out, cnt_out, cnt_scr):
    first = jnp.logical_and(pl.program_id(0) == 0, pl.program_id(1) == 0)

    @pl.when(first)
    def _():
        cnt_scr[...] = jnp.zeros_like(cnt_scr)

    ts = x_ref.shape[1]
    half = A_HEADS * A_HEAD_DIM
    oa = oa_ref[0].reshape(half, ts)
    ob = ob_ref[0].reshape(half, ts)
    x1 = x_ref[0] + _tn(oa, w_out[:half, :]) + _tn(ob, w_out[half:, :])

    h2 = (x1 * lax.rsqrt(jnp.mean(x1 * x1, axis=-1, keepdims=True) + EPS) * g_memx[...]).astype(BF16)
    qm = _nn(h2, m_w_q[...])
    mscale = M_HEAD_DIM ** -0.5 * LOG2E
    heads = []
    for hh in range(M_HEADS):
        sl = slice(hh * M_HEAD_DIM, (hh + 1) * M_HEAD_DIM)
        q = qm[:, sl]
        qn = (q * lax.rsqrt(jnp.mean(q * q, axis=-1, keepdims=True) + EPS) * g_mq[...] * mscale).astype(BF16)
        s = _nt(qn, mk_ref[0, :, sl])
        p = jnp.exp2(s - jnp.max(s, axis=-1, keepdims=True))
        o = _nn(p.astype(BF16), mv_ref[0, :, sl]) * (1.0 / jnp.sum(p, axis=-1, keepdims=True))
        heads.append(o.astype(BF16))
    x2 = x1 + _nn(jnp.concatenate(heads, axis=-1), m_w_o[...])
    x_out[0] = x2

    h3 = x2 * lax.rsqrt(jnp.mean(x2 * x2, axis=-1, keepdims=True) + EPS) * g_ffn[...]
    d_half = h3.shape[1] // 2
    h_out[0] = _pack_bf16_pairs(h3[:, :d_half], h3[:, d_half:])

    h_hi = h3.astype(BF16)
    h_lo = (h3 - h_hi.astype(F32)).astype(BF16)
    logits = _nn(h_hi, r_hi[...]) + _nn(h_hi, r_lo[...]) + _nn(h_lo, r_hi[...]) + r_b[...]
    lane = lax.broadcasted_iota(jnp.int32, (ts, LANES), 1)
    neg = jnp.float32(-jnp.inf)
    lc = jnp.where(lane < N_GROUPS, logits[:, :LANES], neg)
    mc = jnp.max(lc, axis=-1, keepdims=True)
    g_idx = jnp.min(jnp.where(lc == mc, lane, LANES), axis=-1, keepdims=True)
    g_w = 1.0 / jnp.sum(jnp.exp(lc - mc), axis=-1, keepdims=True)
    lf = jnp.where((lane >> 3) == g_idx, logits[:, LANES:], neg)
    m1 = jnp.max(lf, axis=-1, keepdims=True)
    i1 = jnp.min(jnp.where(lf == m1, lane, LANES), axis=-1, keepdims=True)
    lf2 = jnp.where(lane == i1, neg, lf)
    m2 = jnp.max(lf2, axis=-1, keepdims=True)
    i2 = jnp.min(jnp.where(lf2 == m2, lane, LANES), axis=-1, keepdims=True)
    r21 = jnp.exp(m2 - m1)
    gate1 = g_w / (1.0 + r21)
    gate2 = g_w * r21 / (1.0 + r21)

    hot1 = lane == i1
    hot2 = lane == i2
    onehot = jnp.where(jnp.logical_or(hot1, hot2), 1.0, 0.0)
    row = lax.broadcasted_iota(jnp.int32, (ts, ts), 0)
    col = lax.broadcasted_iota(jnp.int32, (ts, ts), 1)
    tri = jnp.where(col < row, 1.0, 0.0).astype(BF16)
    base = _nn(tri, onehot.astype(BF16)) + cnt_scr[...]
    rank1 = jnp.sum(jnp.where(hot1, base, 0.0), axis=-1, keepdims=True)
    rank2 = jnp.sum(jnp.where(hot2, base, 0.0), axis=-1, keepdims=True)
    cnt_scr[...] += jnp.sum(onehot, axis=0, keepdims=True)
    cnt_out[...] = cnt_scr[...]

    vals = (i1.astype(F32), i2.astype(F32), gate1, gate2, rank1, rank2)
    route = jnp.zeros((ts, LANES), F32)
    for n, v in enumerate(vals):
        route = jnp.where(lane == n, v, route)
    route_out[0] = route


def _post_attn(x, oa, ob, mk, mv, lw, ts):
    bsz, seq, d_model = x.shape
    n_mem, hw = mk.shape[1:]
    grid = (bsz, seq // ts)
    full = lambda a: pl.BlockSpec(a.shape, lambda b, j: (0,) * a.ndim)
    ws = [lw[k] for k in ('w_out', 'g_memx', 'm_w_q', 'g_mq', 'm_w_o', 'g_ffn', 'r_hi', 'r_lo', 'r_b')]
    tok = lambda w: pl.BlockSpec((1, ts, w), lambda b, j: (b, j, 0))
    return pl.pallas_call(
        _post_attn_kernel, grid=grid,
        in_specs=[tok(d_model),
                  pl.BlockSpec((1, A_HEADS, A_HEAD_DIM, ts), lambda b, j: (b, 0, 0, j)),
                  pl.BlockSpec((1, B_HEADS, B_V, ts), lambda b, j: (b, 0, 0, j)),
                  pl.BlockSpec((1, n_mem, hw), lambda b, j: (b, 0, 0)),
                  pl.BlockSpec((1, n_mem, hw), lambda b, j: (b, 0, 0))] + [full(w) for w in ws],
        out_specs=[tok(d_model), tok(d_model // 2), tok(LANES), pl.BlockSpec((1, LANES), lambda b, j: (0, 0))],
        out_shape=[jax.ShapeDtypeStruct((bsz, seq, d_model), F32),
                   jax.ShapeDtypeStruct((bsz, seq, d_model // 2), jnp.uint32),
                   jax.ShapeDtypeStruct((bsz, seq, LANES), F32),
                   jax.ShapeDtypeStruct((1, LANES), F32)],
        scratch_shapes=[pltpu.VMEM((1, LANES), F32)],
        compiler_params=_cparams(("arbitrary", "arbitrary")), name="post_attn",
    )(x, oa, ob, mk, mv, *ws)


def _row_copy(src, dst, sem, s_row, d_row):
    return pltpu.make_async_copy(src.at[pl.ds(s_row, 1)], dst.at[pl.ds(d_row, 1)], sem)


def _dispatch_kernel(dest_ref, h_ref, xs_in, xs_out, sem, *, td):
    del xs_in

    def issue(t, c):
        for kk in range(TOP_K):
            _row_copy(h_ref, xs_out, sem, t, dest_ref[0, 0, TOP_K * t + kk]).start(priority=kk)
        return c

    lax.fori_loop(0, td, issue, 0, unroll=ISSUE_UNROLL)
    for kk in range(TOP_K):
        pltpu.make_async_copy(h_ref, xs_out.at[pl.ds(0, td)], sem).wait()


def _dispatch(h_packed, dest, n_rows, td):
    n_tok, width = h_packed.shape
    dest3 = dest.reshape(n_tok // td, 1, TOP_K * td)
    xs0 = jnp.zeros((n_rows, width), h_packed.dtype)
    return pl.pallas_call(
        functools.partial(_dispatch_kernel, td=td), grid=(n_tok // td,),
        in_specs=[pl.BlockSpec((1, 1, TOP_K * td), lambda i: (i, 0, 0), memory_space=pltpu.SMEM),
                  pl.BlockSpec((td, width), lambda i: (i, 0)), pl.BlockSpec(memory_space=pl.ANY)],
        out_specs=pl.BlockSpec(memory_space=pl.ANY),
        out_shape=jax.ShapeDtypeStruct((n_rows, width), h_packed.dtype),
        scratch_shapes=[pltpu.SemaphoreType.DMA(())],
        input_output_aliases={2: 0},
        compiler_params=_cparams(("arbitrary",)), name="moe_dispatch",
    )(dest3, h_packed, xs0)


def _experts_kernel(be_ref, nu_ref, xs_ref, w_gu, w_dn, ys_ref):
    i = pl.program_id(0)

    @pl.when(i < nu_ref[0])
    def _():
        packed = xs_ref[...]
        lo = lax.bitcast_convert_type(packed << 16, F32)
        hi = lax.bitcast_convert_type(packed & jnp.uint32(0xFFFF0000), F32)
        xb = jnp.concatenate([lo, hi], axis=-1).astype(BF16)
        gu = _nn(xb, w_gu[0])
        d_e = gu.shape[1] // 2
        g = gu[:, :d_e]
        a = g * (1.0 / (1.0 + jnp.exp(-g))) * gu[:, d_e:]
        ys_ref[...] = _nn(a.astype(BF16), w_dn[0])

    @pl.when(i >= nu_ref[0])
    def _():
        ys_ref[...] = jnp.zeros_like(ys_ref)


def _experts(xs, block_expert, n_used, lw, bm):
    n_rows, width = xs.shape
    d_model = 2 * width
    w_gu, w_dn = lw['w_gu'], lw['w_dn']
    grid_spec = pltpu.PrefetchScalarGridSpec(
        num_scalar_prefetch=2, grid=(n_rows // bm,),
        in_specs=[pl.BlockSpec((bm, width), lambda i, be, nu: (i, 0)),
                  pl.BlockSpec((1,) + w_gu.shape[1:], lambda i, be, nu: (be[i], 0, 0)),
                  pl.BlockSpec((1,) + w_dn.shape[1:], lambda i, be, nu: (be[i], 0, 0))],
        out_specs=pl.BlockSpec((bm, d_model), lambda i, be, nu: (i, 0)))
    return pl.pallas_call(
        _experts_kernel, grid_spec=grid_spec,
        out_shape=jax.ShapeDtypeStruct((n_rows, d_model), F32),
        compiler_params=_cparams(("arbitrary",)), name="moe_experts",
    )(block_expert, n_used, xs, w_gu, w_dn)


def _combine_kernel(dest_ref, ys_ref, x_ref, route_ref, o_ref, buf, sem, *, tc):
    def issue(t, c):
        for kk in range(TOP_K):
            pltpu.make_async_copy(ys_ref.at[pl.ds(dest_ref[0, 0, TOP_K * t + kk], 1)],
                                  buf.at[kk, pl.ds(t, 1)], sem).start(priority=kk)
        return c

    lax.fori_loop(0, tc, issue, 0, unroll=ISSUE_UNROLL)
    for kk in range(TOP_K):
        pltpu.make_async_copy(ys_ref.at[pl.ds(0, tc)], buf.at[kk], sem).wait()
    route = route_ref[...]
    o_ref[...] = x_ref[...] + route[:, 2:3] * buf[0] + route[:, 3:4] * buf[1]


def _combine(ys, dest, x, route, tc):
    n_tok, d_model = x.shape
    dest3 = dest.reshape(n_tok // tc, 1, TOP_K * tc)
    return pl.pallas_call(
        functools.partial(_combine_kernel, tc=tc), grid=(n_tok // tc,),
        in_specs=[pl.BlockSpec((1, 1, TOP_K * tc), lambda i: (i, 0, 0), memory_space=pltpu.SMEM),
                  pl.BlockSpec(memory_space=pl.ANY),
                  pl.BlockSpec((tc, d_model), lambda i: (i, 0)),
                  pl.BlockSpec((tc, LANES), lambda i: (i, 0))],
        out_specs=pl.BlockSpec((tc, d_model), lambda i: (i, 0)),
        out_shape=jax.ShapeDtypeStruct((n_tok, d_model), F32),
        scratch_shapes=[pltpu.VMEM((TOP_K, tc, d_model), F32), pltpu.SemaphoreType.DMA(())],
        compiler_params=_cparams(("arbitrary",)), name="moe_combine",
    )(dest3, ys, x, route)


def _moe(x2, h_packed, route, counts, lw, cfg):
    bsz, seq, d_model = x2.shape
    n_tok = bsz * seq
    bm = cfg['bm']
    cnt = counts[0, :N_EXPERTS].astype(jnp.int32)
    padded = ((cnt + bm - 1) // bm) * bm
    ends = jnp.cumsum(padded)
    starts = ends - padded
    n_blk = -(-(n_tok * TOP_K + N_EXPERTS * (bm - 1)) // bm)
    route2 = route.reshape(n_tok, LANES)
    expert = route2[:, 0:TOP_K].astype(jnp.int32)
    rank = route2[:, 4:4 + TOP_K].astype(jnp.int32)
    e_ids = jnp.arange(N_EXPERTS, dtype=jnp.int32)
    start_of = jnp.sum(jnp.where(expert[..., None] == e_ids, starts, 0), axis=-1)
    dest = (start_of + rank).reshape(-1)
    blk_row = jnp.arange(n_blk, dtype=jnp.int32) * bm
    block_expert = jnp.minimum(jnp.sum((ends[None, :] <= blk_row[:, None]).astype(jnp.int32), axis=-1), N_EXPERTS - 1)
    n_used = (ends[-1:] // bm).astype(jnp.int32)
    xs = _dispatch(h_packed.reshape(n_tok, d_model // 2), dest, n_blk * bm, cfg['td'])
    ys = _experts(xs, block_expert, n_used, lw, bm)
    out = _combine(ys, dest, x2.reshape(n_tok, d_model), route2, cfg['tc'])
    return out.reshape(bsz, seq, d_model)


def _config(seq):
    if seq % 2048 == 0:
        tk, unroll = (512, 4) if seq >= 8192 else (256, 2)
        return dict(ts=512, tq_a=512, tq_b=2048, tk=tk, wp=256, interleave=True, unroll=unroll,
                    bm=512, td=2048, tc=1024)
    return dict(ts=128, tq_a=128, tq_b=256, tk=64, wp=128, interleave=True, unroll=1,
                bm=64, td=64, tc=64)


def _trunk(x, mem, layers, cfg):
    rt = _rope_tables(x.shape[1])
    for lw in layers:
        qa, ka, va, qb, kb, vb = _in_proj(x, lw, rt, cfg['ts'])
        att = dict(tk=cfg['tk'], wp=cfg['wp'], interleave=cfg['interleave'], unroll=cfg['unroll'])
        oa = _attention(qa, ka, va, dq=A_HEAD_DIM, tqb=cfg['tq_a'], name="attn_a", **att)
        ob = _attention(qb, kb, vb, dq=B_QK, tqb=cfg['tq_b'], name="attn_b", **att)
        mk, mv = _mem_kv(mem, lw)
        x2, h_packed, route, counts = _post_attn(x, oa, ob, mk, mv, lw, cfg['ts'])
        x = _moe(x2, h_packed, route, counts, lw, cfg)
    return x


def kernel(x_prompt, x_sample, mem_prompt, mem_sample, norm_mix, w_in, a_q_norm, a_k_norm, b_cq_norm, b_ckv_norm, b_w_uq, b_w_ukv, b_q_norm, b_k_norm, w_out, norm_mem_x, norm_mem_m, m_w_q, m_w_kv, m_q_norm, m_k_norm, m_w_o, norm_ffn, r_coarse_w, r_coarse_b, r_fine_w, r_fine_b, e_w_gate, e_w_up, e_w_down):
    p = dict(norm_mix=norm_mix, w_in=w_in, a_q_norm=a_q_norm, a_k_norm=a_k_norm, b_cq_norm=b_cq_norm,
             b_ckv_norm=b_ckv_norm, b_w_uq=b_w_uq, b_w_ukv=b_w_ukv, b_q_norm=b_q_norm, b_k_norm=b_k_norm,
             w_out=w_out, norm_mem_x=norm_mem_x, norm_mem_m=norm_mem_m, m_w_q=m_w_q, m_w_kv=m_w_kv,
             m_q_norm=m_q_norm, m_k_norm=m_k_norm, m_w_o=m_w_o, norm_ffn=norm_ffn, r_coarse_w=r_coarse_w,
             r_coarse_b=r_coarse_b, r_fine_w=r_fine_w, r_fine_b=r_fine_b, e_w_gate=e_w_gate, e_w_up=e_w_up,
             e_w_down=e_w_down)
    layers = [_prep_layer(p, i) for i in range(w_in.shape[0])]
    y_prompt = _trunk(x_prompt, mem_prompt, layers, _config(x_prompt.shape[1]))
    y_sample = _trunk(x_sample, mem_sample, layers, _config(x_sample.shape[1]))
    return (y_prompt, y_sample)
```
